```python
import math
import jax, jax.numpy as jnp
from jax import lax
import numpy as np

D_MODEL = 2048
BATCH = 4
SEQ = 2048
DEPTH = 2

CHUNK = 64
N_PREV_CHUNKS = 8
BAND = (N_PREV_CHUNKS + 1) * CHUNK
ATT_HEADS = 8
ATT_HEAD_DIM = D_MODEL // 16
ATT_WIDTH = ATT_HEADS * ATT_HEAD_DIM
REL_CLIP = 128
N_REL = 2 * REL_CLIP + 1
SGU_GROUPS = 8
SGU_GROUP_DIM = D_MODEL // 16
SGU_WIDTH = SGU_GROUPS * SGU_GROUP_DIM
SGU_WINDOW = 128
MIX_WIDTH = ATT_WIDTH + SGU_WIDTH
IN_WIDTH = 3 * ATT_WIDTH + 2 * SGU_WIDTH
N_EXPERTS = 16
N_EXPERT_GROUPS = 4
EXPERTS_PER_GROUP = N_EXPERTS // N_EXPERT_GROUPS
TOP_K = 2
D_FF_EXPERT = D_MODEL // 2
DEEPNORM_ALPHA = (2.0 * DEPTH) ** 0.25
DEEPNORM_BETA = (8.0 * DEPTH) ** -0.25
LN_EPS = 1e-5
NEG_INF = -1e30

kernel_name = "chunk_hybrid_relattn_sgu_grouped_moe_deepnorm"


def layer_norm(x, g, b):
    xf = x.astype(jnp.float32)
    mu = jnp.mean(xf, axis=-1, keepdims=True)
    xc = xf - mu
    var = jnp.mean(jnp.square(xc), axis=-1, keepdims=True)
    y = xc * lax.rsqrt(var + LN_EPS) * g.astype(jnp.float32) + b.astype(jnp.float32)
    return y.astype(x.dtype)


def _rel_index():
    i = np.arange(CHUNK)[:, None]
    key_pos = np.arange(BAND)[None, :] - N_PREV_CHUNKS * CHUNK
    return np.clip(i - key_pos, -REL_CLIP, REL_CLIP) + REL_CLIP


def _band_valid(n_chunks):
    j = np.arange(BAND) // CHUNK
    return (np.arange(n_chunks)[:, None] - N_PREV_CHUNKS + j[None, :]) >= 0


def chunk_band_attention(q, k, v, rel_table):
    b, s, h, dh = q.shape
    nc = s // CHUNK
    qc = q.reshape(b, nc, CHUNK, h, dh)

    def band(t):
        tc = t.reshape(b, nc, CHUNK, h, dh)
        tp = jnp.pad(tc, ((0, 0), (N_PREV_CHUNKS, 0), (0, 0), (0, 0), (0, 0)))
        return jnp.concatenate([tp[:, j:j + nc] for j in range(N_PREV_CHUNKS + 1)], axis=2)

    kb, vb = band(k), band(v)
    scores = jnp.einsum('bnqhd,bnkhd->bnhqk', qc.astype(jnp.float32), kb.astype(jnp.float32))
    scores = scores * (1.0 / math.sqrt(dh))
    bias = rel_table.astype(jnp.float32)[:, _rel_index()]
    scores = scores + bias[None, None]
    valid = jnp.asarray(_band_valid(nc))[None, :, None, None, :]
    scores = jnp.where(valid, scores, NEG_INF)
    p = jax.nn.softmax(scores, axis=-1).astype(v.dtype)
    out = jnp.einsum('bnhqk,bnkhd->bnqhd', p, vb)
    return out.reshape(b, s, h * dh)


def spatial_gating(u, v, ln_g, ln_b, w_s, b_s):
    b, s, _ = u.shape
    nb = s // SGU_WINDOW
    v = layer_norm(v, ln_g, ln_b)
    vb = v.reshape(b, nb, SGU_WINDOW, SGU_GROUPS, SGU_GROUP_DIM)
    causal = jnp.asarray(np.tril(np.ones((SGU_WINDOW, SGU_WINDOW), dtype=bool)))
    w_m = jnp.where(causal[None], w_s, jnp.zeros((), w_s.dtype))
    mixed = jnp.einsum('gts,bnsgc->bntgc', w_m, vb) + b_s.T[None, None, :, :, None]
    return u * mixed.reshape(b, s, SGU_WIDTH)


def grouped_top2_moe(h, router_w, router_b, w_gate, w_up, w_down):
    b, s, d = h.shape
    t = h.reshape(b * s, d)
    logits = t.astype(jnp.float32) @ router_w.astype(jnp.float32) + router_b.astype(jnp.float32)
    probs = jax.nn.softmax(logits, axis=-1)
    pg = probs.reshape(-1, N_EXPERT_GROUPS, EXPERTS_PER_GROUP)
    group_score = lax.top_k(pg, TOP_K)[0].sum(-1)
    sel_group = jnp.argmax(group_score, axis=-1)
    in_group = (jnp.arange(N_EXPERT_GROUPS)[None, :] == sel_group[:, None])[:, :, None]
    masked = jnp.where(in_group, pg, -1.0).reshape(-1, N_EXPERTS)
    top_p, top_i = lax.top_k(masked, TOP_K)
    gates = top_p / jnp.sum(top_p, axis=-1, keepdims=True)
    combine = jnp.sum(jax.nn.one_hot(top_i, N_EXPERTS, dtype=jnp.float32) * gates[..., None], axis=1)
    out = jnp.zeros((b * s, d), jnp.float32)
    for e in range(N_EXPERTS):
        hid = jax.nn.silu(t @ w_gate[e]) * (t @ w_up[e])
        out = out + combine[:, e:e + 1] * (hid @ w_down[e]).astype(jnp.float32)
    return out.astype(h.dtype).reshape(b, s, d)


def setup_inputs(seed: int = 0) -> dict:
    key = jax.random.key(seed)
    ks = jax.random.split(key, 20)
    f32 = jnp.float32
    L = DEPTH
    col_scale = jnp.concatenate([
        jnp.ones((2 * ATT_WIDTH,), f32),
        jnp.full((ATT_WIDTH + 2 * SGU_WIDTH,), DEEPNORM_BETA, f32)])
    x = jax.random.normal(ks[0], (BATCH, SEQ, D_MODEL), f32)
    ln_in_g = 1.0 + 0.02 * jax.random.normal(ks[1], (D_MODEL,), f32)
    ln_in_b = 0.02 * jax.random.normal(ks[2], (D_MODEL,), f32)
    w_in = jax.random.normal(ks[3], (L, D_MODEL, IN_WIDTH), f32) * (D_MODEL ** -0.5) * col_scale
    rel_bias = 0.1 * jax.random.normal(ks[4], (L, ATT_HEADS, N_REL), f32)
    sgu_ln_g = 1.0 + 0.02 * jax.random.normal(ks[5], (L, SGU_WIDTH), f32)
    sgu_ln_b = 0.02 * jax.random.normal(ks[6], (L, SGU_WIDTH), f32)
    sgu_w = jax.random.normal(ks[7], (L, SGU_GROUPS, SGU_WINDOW, SGU_WINDOW), f32) * (0.5 * SGU_WINDOW ** -0.5)
    sgu_b = 1.0 + 0.02 * jax.random.normal(ks[8], (L, SGU_GROUPS, SGU_WINDOW), f32)
    w_out = jax.random.normal(ks[9], (L, MIX_WIDTH, D_MODEL), f32) * (DEEPNORM_BETA * MIX_WIDTH ** -0.5)
    ln1_g = 1.0 + 0.02 * jax.random.normal(ks[10], (L, D_MODEL), f32)
    ln1_b = 0.02 * jax.random.normal(ks[11], (L, D_MODEL), f32)
    router_w = jax.random.normal(ks[12], (D_MODEL, N_EXPERTS), f32) * (D_MODEL ** -0.5)
    router_b = 0.01 * jax.random.normal(ks[13], (N_EXPERTS,), f32)
    w_gate = jax.random.normal(ks[14], (L, N_EXPERTS, D_MODEL, D_FF_EXPERT), f32) * (DEEPNORM_BETA * D_MODEL ** -0.5)
    w_up = jax.random.normal(ks[15], (L, N_EXPERTS, D_MODEL, D_FF_EXPERT), f32) * (DEEPNORM_BETA * D_MODEL ** -0.5)
    w_down = jax.random.normal(ks[16], (L, N_EXPERTS, D_FF_EXPERT, D_MODEL), f32) * (DEEPNORM_BETA * D_FF_EXPERT ** -0.5)
    ln2_g = 1.0 + 0.02 * jax.random.normal(ks[17], (L, D_MODEL), f32)
    ln2_b = 0.02 * jax.random.normal(ks[18], (L, D_MODEL), f32)
    return {"x": x, "ln_in_g": ln_in_g, "ln_in_b": ln_in_b, "w_in": w_in, "rel_bias": rel_bias,
            "sgu_ln_g": sgu_ln_g, "sgu_ln_b": sgu_ln_b, "sgu_w": sgu_w, "sgu_b": sgu_b,
            "w_out": w_out, "ln1_g": ln1_g, "ln1_b": ln1_b, "router_w": router_w,
            "router_b": router_b, "w_gate": w_gate, "w_up": w_up, "w_down": w_down,
            "ln2_g": ln2_g, "ln2_b": ln2_b}


def reference(x, ln_in_g, ln_in_b, w_in, rel_bias, sgu_ln_g, sgu_ln_b, sgu_w, sgu_b,
              w_out, ln1_g, ln1_b, router_w, router_b, w_gate, w_up, w_down, ln2_g, ln2_b):
    b, s, _ = x.shape
    h = layer_norm(x, ln_in_g, ln_in_b)
    splits = [ATT_WIDTH, 2 * ATT_WIDTH, 3 * ATT_WIDTH, 3 * ATT_WIDTH + SGU_WIDTH]
    for l in range(DEPTH):
        z = h @ w_in[l]
        q, k, v, u_s, v_s = jnp.split(z, splits, axis=-1)
        hd = (b, s, ATT_HEADS, ATT_HEAD_DIM)
        y_att = chunk_band_attention(q.reshape(hd), k.reshape(hd), v.reshape(hd), rel_bias[l])
        y_sgu = spatial_gating(jax.nn.gelu(u_s, approximate=False), jax.nn.gelu(v_s, approximate=False),
                               sgu_ln_g[l], sgu_ln_b[l], sgu_w[l], sgu_b[l])
        mix = jnp.concatenate([y_att, y_sgu], axis=-1) @ w_out[l]
        h = layer_norm(DEEPNORM_ALPHA * h + mix, ln1_g[l], ln1_b[l])
        ffn = grouped_top2_moe(h, router_w, router_b, w_gate[l], w_up[l], w_down[l])
        h = layer_norm(DEEPNORM_ALPHA * h + ffn, ln2_g[l], ln2_b[l])
    return h
```

```python
import functools
import math

import jax
import jax.numpy as jnp
import numpy as np
from jax import lax
from jax.experimental import pallas as pl
from jax.experimental.pallas import tpu as pltpu

D = 2048
HEADS = 8
DH = 128
ATT_W = HEADS * DH
SGU_G = 8
SGU_W = SGU_G * 128
WIN = 128
IN_W = 3 * ATT_W + 2 * SGU_W
CHUNK = 64
N_PREV = 8
REL_CLIP = 128
N_EXP = 16
N_GRP = 4
EPG = 4
F = 1024
DEPTH = 2
ALPHA = (2.0 * DEPTH) ** 0.25
EPS = 1e-5
NEG = -1e30

LANES = 128
SUBLANES = 8
VMEM_LIMIT = 56 * 1024 * 1024

QB = 128
KWIN = QB + N_PREV * CHUNK
TM = 256
WPT = D // (2 * LANES)


def _cp(sem, vmem=VMEM_LIMIT):
    return pltpu.CompilerParams(dimension_semantics=sem, vmem_limit_bytes=vmem)


def _ln(x, g, b):
    mu = jnp.mean(x, axis=-1, keepdims=True)
    xc = x - mu
    var = jnp.mean(xc * xc, axis=-1, keepdims=True)
    return xc * lax.rsqrt(var + EPS) * g + b


def _pack_rows(y, o_ref):
    t = y.shape[0]
    bits = pltpu.bitcast(y.astype(jnp.bfloat16).astype(jnp.float32), jnp.uint32)
    for s in range(WPT):
        lo = bits[:, 2 * LANES * s:2 * LANES * s + LANES]
        hi = bits[:, 2 * LANES * s + LANES:2 * LANES * (s + 1)]
        o_ref[pl.ds(s, t, stride=WPT), :] = (lo >> 16) | (hi & jnp.uint32(0xFFFF0000))


def _unpack_rows(p_ref, t):
    chunks = []
    for s in range(WPT):
        w = p_ref[pl.ds(s, t, stride=WPT), :]
        chunks.append(pltpu.bitcast(w << 16, jnp.float32))
        chunks.append(pltpu.bitcast(w & jnp.uint32(0xFFFF0000), jnp.float32))
    return chunks


def _ln0_body(x_ref, g_ref, b_ref, hf_ref, hb_ref):
    y = _ln(x_ref[...], g_ref[...], b_ref[...])
    hf_ref[...] = y
    hb_ref[...] = y.astype(jnp.bfloat16)


def _ln0(x, g, b):
    n = x.shape[0]
    t = 256
    return pl.pallas_call(
        _ln0_body,
        grid=(n // t,),
        in_specs=[pl.BlockSpec((t, D), lambda i: (i, 0)),
                  pl.BlockSpec((1, D), lambda i: (0, 0)),
                  pl.BlockSpec((1, D), lambda i: (0, 0))],
        out_specs=[pl.BlockSpec((t, D), lambda i: (i, 0)),
                   pl.BlockSpec((t, D), lambda i: (i, 0))],
        out_shape=[jax.ShapeDtypeStruct((n, D), jnp.float32),
                   jax.ShapeDtypeStruct((n, D), jnp.bfloat16)],
        compiler_params=_cp(("arbitrary",)),
        name="ln_in",
    )(x, g.reshape(1, D), b.reshape(1, D))


def _inproj_body(x_ref, w_ref, o_ref, wb_ref):
    @pl.when(pl.program_id(1) == 0)
    def _():
        wb_ref[...] = w_ref[...].astype(jnp.bfloat16)

    o_ref[...] = jnp.dot(x_ref[...], wb_ref[...],
                         preferred_element_type=jnp.float32).astype(o_ref.dtype)


def _inproj(hb, w):
    n = hb.shape[0]
    tm, tn = 1024, 1024
    return pl.pallas_call(
        _inproj_body,
        grid=(IN_W // tn, n // tm),
        in_specs=[pl.BlockSpec((tm, D), lambda j, i: (i, 0)),
                  pl.BlockSpec((D, tn), lambda j, i: (0, j))],
        out_specs=pl.BlockSpec((tm, tn), lambda j, i: (i, j)),
        out_shape=jax.ShapeDtypeStruct((n, IN_W), jnp.bfloat16),
        scratch_shapes=[pltpu.VMEM((D, tn), jnp.bfloat16)],
        compiler_params=_cp(("arbitrary", "arbitrary")),
        name="in_proj",
    )(hb, w)


def _attn_body(q_ref, k_ref, v_ref, bm_ref, o_ref):
    seq = q_ref.shape[0]
    scale = 1.0 / math.sqrt(DH)
    for j in range(seq // QB):
        q0 = j * QB
        k0 = max(0, q0 - N_PREV * CHUNK)
        nk = q0 + QB - k0
        q = q_ref[q0:q0 + QB, :]
        k = k_ref[k0:k0 + nk, :]
        v = v_ref[k0:k0 + nk, :]
        s = lax.dot_general(q, k, (((1,), (1,)), ((), ())),
                            preferred_element_type=jnp.float32)
        s = s * scale + bm_ref[0, :, KWIN - nk:]
        m = jnp.max(s, axis=-1, keepdims=True)
        p = jnp.exp(s - m)
        l = jnp.sum(p, axis=-1, keepdims=True)
        o = jnp.dot(p.astype(jnp.bfloat16), v, preferred_element_type=jnp.float32)
        o_ref[q0:q0 + QB, :] = (o / l).astype(o_ref.dtype)


def _attn_bias(rel_table):
    i = np.arange(QB)[:, None]
    c = np.arange(KWIN)[None, :]
    rel = np.clip(i - (c - N_PREV * CHUNK), -REL_CLIP, REL_CLIP) + REL_CLIP
    qc = i // CHUNK
    kc = c // CHUNK
    band = (kc >= qc) & (kc <= qc + N_PREV)
    bias = rel_table.astype(jnp.float32)[:, rel]
    return jnp.where(jnp.asarray(band)[None], bias, NEG)


def _attention(z, bm, batch, seq):
    return pl.pallas_call(
        _attn_body,
        grid=(batch, HEADS),
        in_specs=[pl.BlockSpec((seq, DH), lambda b, h: (b, h)),
                  pl.BlockSpec((seq, DH), lambda b, h: (b, HEADS + h)),
                  pl.BlockSpec((seq, DH), lambda b, h: (b, 2 * HEADS + h)),
                  pl.BlockSpec((1, QB, KWIN), lambda b, h: (h, 0, 0))],
        out_specs=pl.BlockSpec((seq, DH), lambda b, h: (b, h)),
        out_shape=jax.ShapeDtypeStruct((batch * seq, ATT_W), jnp.bfloat16),
        compiler_params=_cp(("arbitrary", "arbitrary")),
        name="band_attn",
    )(z, z, z, bm)


def _gelu(x):
    return 0.5 * x * (1.0 + lax.erf(x * (1.0 / math.sqrt(2.0))))


def _sgu_body(u_ref, v_ref, g_ref, b_ref, w_ref, bs_ref, o_ref):
    t = u_ref.shape[0]
    u = _gelu(u_ref[...].astype(jnp.float32))
    v = _gelu(v_ref[...].astype(jnp.float32))
    vn = _ln(v, g_ref[...], b_ref[...]).astype(jnp.bfloat16)
    row = lax.broadcasted_iota(jnp.int32, (WIN, WIN), 0)
    col = lax.broadcasted_iota(jnp.int32, (WIN, WIN), 1)
    causal = row >= col
    for g in range(SGU_G):
        wm = jnp.where(causal, w_ref[g], 0.0).astype(jnp.bfloat16)
        bias = bs_ref[:, g:g + 1]
        for n in range(t // WIN):
            r0, c0 = n * WIN, g * 128
            mixed = jnp.dot(wm, vn[r0:r0 + WIN, c0:c0 + 128],
                            preferred_element_type=jnp.float32) + bias
            o_ref[r0:r0 + WIN, c0:c0 + 128] = (u[r0:r0 + WIN, c0:c0 + 128] * mixed
                                               ).astype(o_ref.dtype)


def _sgu(z, ln_g, ln_b, w_s, b_s):
    n = z.shape[0]
    t = 256
    ublk = 3 * ATT_W // SGU_W
    return pl.pallas_call(
        _sgu_body,
        grid=(n // t,),
        in_specs=[pl.BlockSpec((t, SGU_W), lambda i: (i, ublk)),
                  pl.BlockSpec((t, SGU_W), lambda i: (i, ublk + 1)),
                  pl.BlockSpec((1, SGU_W), lambda i: (0, 0)),
                  pl.BlockSpec((1, SGU_W), lambda i: (0, 0)),
                  pl.BlockSpec((SGU_G, WIN, WIN), lambda i: (0, 0, 0)),
                  pl.BlockSpec((WIN, SGU_G), lambda i: (0, 0))],
        out_specs=pl.BlockSpec((t, SGU_W), lambda i: (i, 0)),
        out_shape=jax.ShapeDtypeStruct((n, SGU_W), jnp.bfloat16),
        compiler_params=_cp(("arbitrary",)),
        name="sgu",
    )(z, z, ln_g.reshape(1, SGU_W), ln_b.reshape(1, SGU_W), w_s, b_s.T)


def _top2_of4(a, b, c, d):
    m1, n1 = jnp.maximum(a, b), jnp.minimum(a, b)
    m2, n2 = jnp.maximum(c, d), jnp.minimum(c, d)
    top = jnp.maximum(m1, m2)
    second = jnp.maximum(jnp.minimum(m1, m2), jnp.maximum(n1, n2))
    return top + second


def _route(logits):
    rows = [logits[e:e + 1, :] for e in range(N_EXP)]
    mx = functools.reduce(jnp.maximum, rows)
    ex = [jnp.exp(r - mx) for r in rows]
    den = functools.reduce(lambda a, b: a + b, ex)
    p = [e / den for e in ex]
    score = [_top2_of4(*p[EPG * g:EPG * (g + 1)]) for g in range(N_GRP)]
    best, sel = score[0], jnp.zeros_like(score[0], dtype=jnp.int32)
    for g in range(1, N_GRP):
        better = score[g] > best
        best = jnp.where(better, score[g], best)
        sel = jnp.where(better, g, sel)
    cand = []
    for k in range(EPG):
        val = p[k]
        for g in range(1, N_GRP):
            val = jnp.where(sel == g, p[EPG * g + k], val)
        cand.append(val)
    p1, i1 = cand[0], jnp.zeros_like(sel)
    for k in range(1, EPG):
        better = cand[k] > p1
        p1 = jnp.where(better, cand[k], p1)
        i1 = jnp.where(better, k, i1)
    p2, i2 = jnp.full_like(p1, -1.0), jnp.zeros_like(sel)
    for k in range(EPG):
        better = (cand[k] > p2) & (i1 != k)
        p2 = jnp.where(better, cand[k], p2)
        i2 = jnp.where(better, k, i2)
    tot = p1 + p2
    ids = jnp.concatenate([sel * EPG + i1, sel * EPG + i2], axis=0)
    gates = jnp.concatenate([p1 / tot, p2 / tot], axis=0)
    return ids, gates


def _outproj_body(ya_ref, ys_ref, w_ref, h_ref, g_ref, b_ref, rwh_ref, rwl_ref, rb_ref,
                  hf_ref, hp_ref, id_ref, gt_ref):
    mix = jnp.dot(ya_ref[...], w_ref[:ATT_W, :], preferred_element_type=jnp.float32)
    mix = mix + jnp.dot(ys_ref[...], w_ref[ATT_W:, :], preferred_element_type=jnp.float32)
    h1 = _ln(ALPHA * h_ref[...] + mix, g_ref[...], b_ref[...])
    hf_ref[...] = h1
    _pack_rows(h1, hp_ref)
    hh = h1.astype(jnp.bfloat16)
    hl = (h1 - hh.astype(jnp.float32)).astype(jnp.bfloat16)
    nt = (((1,), (1,)), ((), ()))
    logits = lax.dot_general(rwh_ref[...], hh, nt, preferred_element_type=jnp.float32)
    logits = logits + lax.dot_general(rwh_ref[...], hl, nt, preferred_element_type=jnp.float32)
    logits = logits + lax.dot_general(rwl_ref[...], hh, nt, preferred_element_type=jnp.float32)
    ids, gates = _route(logits + rb_ref[...])
    id_ref[...] = ids
    gt_ref[...] = gates


def _outproj(ya, ys, wb, hf, g, b, rwh, rwl, rb):
    n = ya.shape[0]
    t = 256
    return pl.pallas_call(
        _outproj_body,
        grid=(n // t,),
        in_specs=[pl.BlockSpec((t, ATT_W), lambda i: (i, 0)),
                  pl.BlockSpec((t, SGU_W), lambda i: (i, 0)),
                  pl.BlockSpec((D, D), lambda i: (0, 0)),
                  pl.BlockSpec((t, D), lambda i: (i, 0)),
                  pl.BlockSpec((1, D), lambda i: (0, 0)),
                  pl.BlockSpec((1, D), lambda i: (0, 0)),
                  pl.BlockSpec((N_EXP, D), lambda i: (0, 0)),
                  pl.BlockSpec((N_EXP, D), lambda i: (0, 0)),
                  pl.BlockSpec((N_EXP, 1), lambda i: (0, 0))],
        out_specs=[pl.BlockSpec((t, D), lambda i: (i, 0)),
                   pl.BlockSpec((t * WPT, LANES), lambda i: (i, 0)),
                   pl.BlockSpec((2, t), lambda i: (0, i)),
                   pl.BlockSpec((2, t), lambda i: (0, i))],
        out_shape=[jax.ShapeDtypeStruct((n, D), jnp.float32),
                   jax.ShapeDtypeStruct((n * WPT, LANES), jnp.uint32),
                   jax.ShapeDtypeStruct((2, n), jnp.int32),
                   jax.ShapeDtypeStruct((2, n), jnp.float32)],
        compiler_params=_cp(("arbitrary",)),
        name="out_proj_ln_route",
    )(ya, ys, wb, hf, g.reshape(1, D), b.reshape(1, D), rwh, rwl, rb)


def _gather_body(idx_ref, nblk_ref, src_ref, dst_ref, sem):
    i = pl.program_id(0)
    nblk = nblk_ref[0]

    def wait_block(slot):
        pltpu.make_async_copy(src_ref.at[pl.ds(0, TM)], dst_ref.at[pl.ds(0, TM)],
                              sem.at[slot]).wait()

    @pl.when(i < nblk)
    def _():
        slot = i % 2

        def issue(r, carry):
            row = i * TM + r
            pltpu.make_async_copy(src_ref.at[idx_ref[row]], dst_ref.at[row],
                                  sem.at[slot]).start()
            return carry

        lax.fori_loop(0, TM, issue, 0)

        @pl.when(i > 0)
        def _():
            wait_block(1 - slot)

        @pl.when(i == nblk - 1)
        def _():
            wait_block(slot)

    @pl.when(i >= nblk)
    def _():
        fill = pltpu.make_async_copy(src_ref.at[pl.ds(0, TM)], dst_ref.at[pl.ds(i * TM, TM)],
                                     sem.at[2])
        fill.start()
        fill.wait()


def _gather_rows(idx, nblk, src, max_blocks):
    return pl.pallas_call(
        _gather_body,
        grid_spec=pltpu.PrefetchScalarGridSpec(
            num_scalar_prefetch=2,
            grid=(max_blocks,),
            in_specs=[pl.BlockSpec(memory_space=pl.ANY)],
            out_specs=pl.BlockSpec(memory_space=pl.ANY),
            scratch_shapes=[pltpu.SemaphoreType.DMA((3,))]),
        out_shape=jax.ShapeDtypeStruct((max_blocks * TM, WPT, LANES), jnp.uint32),
        compiler_params=pltpu.CompilerParams(dimension_semantics=("arbitrary",)),
        name="row_gather",
    )(idx, nblk, src)


def _expert_changed(te_ref, i):
    prev = te_ref[jnp.maximum(i - 1, 0)]
    return (i == 0) | (te_ref[i] != prev)


def _ffn_up_body(te_ref, nt_ref, x_ref, wg_ref, wu_ref, o_ref, wgb_ref, wub_ref):
    i = pl.program_id(0)

    @pl.when(i < nt_ref[0])
    def _():
        @pl.when(_expert_changed(te_ref, i))
        def _():
            wgb_ref[...] = wg_ref[0].astype(jnp.bfloat16)
            wub_ref[...] = wu_ref[0].astype(jnp.bfloat16)

        x = jnp.concatenate([c.astype(jnp.bfloat16) for c in _unpack_rows(x_ref, TM)], axis=1)
        g = jnp.dot(x, wgb_ref[...], preferred_element_type=jnp.float32)
        u = jnp.dot(x, wub_ref[...], preferred_element_type=jnp.float32)
        o_ref[...] = (g * jax.nn.sigmoid(g) * u).astype(o_ref.dtype)

    @pl.when(i >= nt_ref[0])
    def _():
        o_ref[...] = jnp.zeros_like(o_ref)


def _ffn_down_body(te_ref, nt_ref, h_ref, wd_ref, o_ref, wdb_ref):
    i = pl.program_id(0)

    @pl.when(i < nt_ref[0])
    def _():
        @pl.when(_expert_changed(te_ref, i))
        def _():
            wdb_ref[...] = wd_ref[0].astype(jnp.bfloat16)

        y = jnp.dot(h_ref[...], wdb_ref[...], preferred_element_type=jnp.float32)
        _pack_rows(y, o_ref)

    @pl.when(i >= nt_ref[0])
    def _():
        o_ref[...] = jnp.zeros_like(o_ref)


def _experts(tile_expert, ntiles, xs, w_gate, w_up, w_down, max_tiles):
    def row_blk(i, te, nt):
        return (jnp.minimum(i, nt[0] - 1), 0)

    def out_blk(i, te, nt):
        return (i, 0)

    def w_blk(i, te, nt):
        return (te[i], 0, 0)

    hid = pl.pallas_call(
        _ffn_up_body,
        grid_spec=pltpu.PrefetchScalarGridSpec(
            num_scalar_prefetch=2,
            grid=(max_tiles,),
            in_specs=[pl.BlockSpec((TM * WPT, LANES), row_blk),
                      pl.BlockSpec((1, D, F), w_blk),
                      pl.BlockSpec((1, D, F), w_blk)],
            out_specs=pl.BlockSpec((TM, F), out_blk),
            scratch_shapes=[pltpu.VMEM((D, F), jnp.bfloat16),
                            pltpu.VMEM((D, F), jnp.bfloat16)]),
        out_shape=jax.ShapeDtypeStruct((max_tiles * TM, F), jnp.bfloat16),
        compiler_params=_cp(("arbitrary",)),
        name="expert_up",
    )(tile_expert, ntiles, xs, w_gate, w_up)
    return pl.pallas_call(
        _ffn_down_body,
        grid_spec=pltpu.PrefetchScalarGridSpec(
            num_scalar_prefetch=2,
            grid=(max_tiles,),
            in_specs=[pl.BlockSpec((TM, F), row_blk),
                      pl.BlockSpec((1, F, D), w_blk)],
            out_specs=pl.BlockSpec((TM * WPT, LANES), out_blk),
            scratch_shapes=[pltpu.VMEM((F, D), jnp.bfloat16)]),
        out_shape=jax.ShapeDtypeStruct((max_tiles * TM * WPT, LANES), jnp.uint32),
        compiler_params=_cp(("arbitrary",)),
        name="expert_down",
    )(tile_expert, ntiles, hid, w_down)


def _combine_body(y0_ref, y1_ref, h_ref, gt_ref, g_ref, b_ref, hf_ref, hb_ref):
    t = h_ref.shape[0]
    y0 = jnp.concatenate(_unpack_rows(y0_ref, t), axis=1)
    y1 = jnp.concatenate(_unpack_rows(y1_ref, t), axis=1)
    gt = gt_ref[...]
    ffn = gt[:, 0:1] * y0 + gt[:, 1:2] * y1
    h2 = _ln(ALPHA * h_ref[...] + ffn, g_ref[...], b_ref[...])
    hf_ref[...] = h2
    hb_ref[...] = h2.astype(jnp.bfloat16)


def _combine(yg, hf, gates_t, g, b):
    n = hf.shape[0]
    t = 256
    nb = n // t
    return pl.pallas_call(
        _combine_body,
        grid=(nb,),
        in_specs=[pl.BlockSpec((t * WPT, LANES), lambda i: (i, 0)),
                  pl.BlockSpec((t * WPT, LANES), lambda i: (nb + i, 0)),
                  pl.BlockSpec((t, D), lambda i: (i, 0)),
                  pl.BlockSpec((t, 2), lambda i: (i, 0)),
                  pl.BlockSpec((1, D), lambda i: (0, 0)),
                  pl.BlockSpec((1, D), lambda i: (0, 0))],
        out_specs=[pl.BlockSpec((t, D), lambda i: (i, 0)),
                   pl.BlockSpec((t, D), lambda i: (i, 0))],
        out_shape=[jax.ShapeDtypeStruct((n, D), jnp.float32),
                   jax.ShapeDtypeStruct((n, D), jnp.bfloat16)],
        compiler_params=_cp(("arbitrary",)),
        name="combine_ln",
    )(yg, yg, hf, gates_t, g.reshape(1, D), b.reshape(1, D))


def _dispatch_plan(ids, n, max_tiles):
    e_flat = ids.reshape(-1)
    onehot = (e_flat[:, None] == jnp.arange(N_EXP, dtype=jnp.int32)[None, :]).astype(jnp.int32)
    csum = jnp.cumsum(onehot, axis=0)
    rank = jnp.take_along_axis(csum, e_flat[:, None], axis=1)[:, 0] - 1
    counts = csum[-1]
    tiles = (counts + TM - 1) // TM
    tile_end = jnp.cumsum(tiles)
    tile_start = tile_end - tiles
    pos = (tile_start * TM)[e_flat] + rank
    tok = jnp.concatenate([jnp.arange(n, dtype=jnp.int32)] * 2)
    src_tok = jnp.zeros((max_tiles * TM,), jnp.int32).at[pos].set(tok)
    tile_expert = jnp.minimum(
        jnp.searchsorted(tile_end, jnp.arange(max_tiles, dtype=jnp.int32), side="right"),
        N_EXP - 1).astype(jnp.int32)
    ntiles = tile_end[-1:].astype(jnp.int32)
    return pos.astype(jnp.int32), src_tok, tile_expert, ntiles


def kernel(x, ln_in_g, ln_in_b, w_in, rel_bias, sgu_ln_g, sgu_ln_b, sgu_w, sgu_b, w_out,
           ln1_g, ln1_b, router_w, router_b, w_gate, w_up, w_down, ln2_g, ln2_b):
    batch, seq, _ = x.shape
    n = batch * seq
    max_tiles = (2 * n) // TM + N_EXP
    rwt = router_w.T
    rwh = rwt.astype(jnp.bfloat16)
    rwl = (rwt - rwh.astype(jnp.float32)).astype(jnp.bfloat16)
    rb = router_b.reshape(N_EXP, 1)

    hf, hb = _ln0(x.reshape(n, D), ln_in_g, ln_in_b)
    for l in range(DEPTH):
        z = _inproj(hb, w_in[l])
        ya = _attention(z, _attn_bias(rel_bias[l]), batch, seq)
        ys = _sgu(z, sgu_ln_g[l], sgu_ln_b[l], sgu_w[l], sgu_b[l])
        hf, hp, ids, gates = _outproj(ya, ys, w_out[l].astype(jnp.bfloat16), hf,
                                      ln1_g[l], ln1_b[l], rwh, rwl, rb)
        pos, src_tok, tile_expert, ntiles = _dispatch_plan(ids, n, max_tiles)
        xs = _gather_rows(src_tok, ntiles, hp.reshape(n, WPT, LANES), max_tiles)
        ysorted = _experts(tile_expert, ntiles, xs.reshape(max_tiles * TM * WPT, LANES),
                           w_gate[l], w_up[l], w_down[l], max_tiles)
        yg = _gather_rows(pos, jnp.full((1,), 2 * n // TM, jnp.int32),
                          ysorted.reshape(max_tiles * TM, WPT, LANES), 2 * n // TM)
        hf, hb = _combine(yg.reshape(2 * n * WPT, LANES), hf, gates.T, ln2_g[l], ln2_b[l])
    return hf.reshape(batch, seq, D)
```

```python
import functools
import math

import jax
import jax.numpy as jnp
import numpy as np
from jax import lax
from jax.experimental import pallas as pl
from jax.experimental.pallas import tpu as pltpu

D = 2048
HEADS = 8
DH = 128
ATT_W = HEADS * DH
SGU_G = 8
SGU_W = SGU_G * 128
WIN = 128
IN_W = 3 * ATT_W + 2 * SGU_W
CHUNK = 64
N_PREV = 8
REL_CLIP = 128
N_EXP = 16
N_GRP = 4
EPG = 4
F = 1024
DEPTH = 2
ALPHA = (2.0 * DEPTH) ** 0.25
EPS = 1e-5
NEG = -1e30

LANES = 128
SUBLANES = 8
VMEM_LIMIT = 56 * 1024 * 1024

QB = 128
KWIN = QB + N_PREV * CHUNK
TM = 256
WPT = D // (2 * LANES)


def _cp(sem, vmem=VMEM_LIMIT):
    return pltpu.CompilerParams(dimension_semantics=sem, vmem_limit_bytes=vmem)


def _ln(x, g, b):
    mu = jnp.mean(x, axis=-1, keepdims=True)
    xc = x - mu
    var = jnp.mean(xc * xc, axis=-1, keepdims=True)
    return xc * lax.rsqrt(var + EPS) * g + b


def _pack_rows(y, o_ref):
    t = y.shape[0]
    bits = pltpu.bitcast(y.astype(jnp.bfloat16).astype(jnp.float32), jnp.uint32)
    for s in range(WPT):
        lo = bits[:, 2 * LANES * s:2 * LANES * s + LANES]
        hi = bits[:, 2 * LANES * s + LANES:2 * LANES * (s + 1)]
        o_ref[pl.ds(s, t, stride=WPT), :] = (lo >> 16) | (hi & jnp.uint32(0xFFFF0000))


def _unpack_rows(p_ref, t):
    chunks = []
    for s in range(WPT):
        w = p_ref[pl.ds(s, t, stride=WPT), :]
        chunks.append(pltpu.bitcast(w << 16, jnp.float32))
        chunks.append(pltpu.bitcast(w & jnp.uint32(0xFFFF0000), jnp.float32))
    return chunks


def _ln0_body(x_ref, g_ref, b_ref, hf_ref, hb_ref):
    y = _ln(x_ref[...], g_ref[...], b_ref[...])
    hf_ref[...] = y
    hb_ref[...] = y.astype(jnp.bfloat16)


def _ln0(x, g, b):
    n = x.shape[0]
    t = 256
    return pl.pallas_call(
        _ln0_body,
        grid=(n // t,),
        in_specs=[pl.BlockSpec((t, D), lambda i: (i, 0)),
                  pl.BlockSpec((1, D), lambda i: (0, 0)),
                  pl.BlockSpec((1, D), lambda i: (0, 0))],
        out_specs=[pl.BlockSpec((t, D), lambda i: (i, 0)),
                   pl.BlockSpec((t, D), lambda i: (i, 0))],
        out_shape=[jax.ShapeDtypeStruct((n, D), jnp.float32),
                   jax.ShapeDtypeStruct((n, D), jnp.bfloat16)],
        compiler_params=_cp(("arbitrary",)),
        name="ln_in",
    )(x, g.reshape(1, D), b.reshape(1, D))


def _inproj_body(x_ref, w_ref, o_ref, wb_ref):
    @pl.when(pl.program_id(1) == 0)
    def _():
        wb_ref[...] = w_ref[0].astype(jnp.bfloat16)

    o_ref[...] = jnp.dot(x_ref[...], wb_ref[...],
                         preferred_element_type=jnp.float32).astype(o_ref.dtype)


def _inproj(hb, w, layer):
    n = hb.shape[0]
    tm, tn = 1024, 1024
    return pl.pallas_call(
        _inproj_body,
        grid=(IN_W // tn, n // tm),
        in_specs=[pl.BlockSpec((tm, D), lambda j, i: (i, 0)),
                  pl.BlockSpec((1, D, tn), lambda j, i: (layer, 0, j))],
        out_specs=pl.BlockSpec((tm, tn), lambda j, i: (i, j)),
        out_shape=jax.ShapeDtypeStruct((n, IN_W), jnp.bfloat16),
        scratch_shapes=[pltpu.VMEM((D, tn), jnp.bfloat16)],
        compiler_params=_cp(("arbitrary", "arbitrary")),
        name="in_proj",
    )(hb, w)


def _attn_body(q_ref, k_ref, v_ref, bm_ref, o_ref):
    seq = q_ref.shape[0]
    scale = 1.0 / math.sqrt(DH)
    for j in range(seq // QB):
        q0 = j * QB
        k0 = max(0, q0 - N_PREV * CHUNK)
        nk = q0 + QB - k0
        q = q_ref[q0:q0 + QB, :]
        k = k_ref[k0:k0 + nk, :]
        v = v_ref[k0:k0 + nk, :]
        s = lax.dot_general(q, k, (((1,), (1,)), ((), ())),
                            preferred_element_type=jnp.float32)
        s = s * scale + bm_ref[0, :, KWIN - nk:]
        m = jnp.max(s, axis=-1, keepdims=True)
        p = jnp.exp(s - m)
        l = jnp.sum(p, axis=-1, keepdims=True)
        o = jnp.dot(p.astype(jnp.bfloat16), v, preferred_element_type=jnp.float32)
        o_ref[q0:q0 + QB, :] = (o / l).astype(o_ref.dtype)


def _attn_bias(rel_table):
    period = 1024
    off = N_PREV * CHUNK
    tab = rel_table.astype(jnp.float32)
    heads = tab.shape[0]
    far = tab[:, 2 * REL_CLIP:]
    n_far = off - REL_CLIP + 1
    near = tab[:, 1:2 * REL_CLIP][:, ::-1]
    g = jnp.concatenate([jnp.broadcast_to(far, (heads, n_far)), near,
                         jnp.broadcast_to(far, (heads, period - n_far - near.shape[1]))], axis=1)
    bias = jnp.tile(g, (1, QB + 1))[:, :QB * (period - 1)].reshape(heads, QB, period - 1)
    bias = bias[:, :, :KWIN]
    i = np.arange(QB)[:, None]
    c = np.arange(KWIN)[None, :]
    qc = i // CHUNK
    kc = c // CHUNK
    band = (kc >= qc) & (kc <= qc + N_PREV)
    return jnp.where(jnp.asarray(band)[None], bias, NEG)


def _attention(z, bm, batch, seq):
    return pl.pallas_call(
        _attn_body,
        grid=(batch, HEADS),
        in_specs=[pl.BlockSpec((seq, DH), lambda b, h: (b, h)),
                  pl.BlockSpec((seq, DH), lambda b, h: (b, HEADS + h)),
                  pl.BlockSpec((seq, DH), lambda b, h: (b, 2 * HEADS + h)),
                  pl.BlockSpec((1, QB, KWIN), lambda b, h: (h, 0, 0))],
        out_specs=pl.BlockSpec((seq, DH), lambda b, h: (b, h)),
        out_shape=jax.ShapeDtypeStruct((batch * seq, ATT_W), jnp.bfloat16),
        compiler_params=_cp(("arbitrary", "arbitrary")),
        name="band_attn",
    )(z, z, z, bm)


def _gelu(x):
    return 0.5 * x * (1.0 + lax.erf(x * (1.0 / math.sqrt(2.0))))


def _sgu_body(u_ref, v_ref, g_ref, b_ref, w_ref, bs_ref, o_ref):
    t = u_ref.shape[0]
    u = _gelu(u_ref[...].astype(jnp.float32))
    v = _gelu(v_ref[...].astype(jnp.float32))
    vn = _ln(v, g_ref[...], b_ref[...]).astype(jnp.bfloat16)
    row = lax.broadcasted_iota(jnp.int32, (WIN, WIN), 0)
    col = lax.broadcasted_iota(jnp.int32, (WIN, WIN), 1)
    causal = row >= col
    for g in range(SGU_G):
        wm = jnp.where(causal, w_ref[g], 0.0).astype(jnp.bfloat16)
        bias = bs_ref[:, g:g + 1]
        for n in range(t // WIN):
            r0, c0 = n * WIN, g * 128
            mixed = jnp.dot(wm, vn[r0:r0 + WIN, c0:c0 + 128],
                            preferred_element_type=jnp.float32) + bias
            o_ref[r0:r0 + WIN, c0:c0 + 128] = (u[r0:r0 + WIN, c0:c0 + 128] * mixed
                                               ).astype(o_ref.dtype)


def _sgu(z, ln_g, ln_b, w_s, b_s):
    n = z.shape[0]
    t = 256
    ublk = 3 * ATT_W // SGU_W
    return pl.pallas_call(
        _sgu_body,
        grid=(n // t,),
        in_specs=[pl.BlockSpec((t, SGU_W), lambda i: (i, ublk)),
                  pl.BlockSpec((t, SGU_W), lambda i: (i, ublk + 1)),
                  pl.BlockSpec((1, SGU_W), lambda i: (0, 0)),
                  pl.BlockSpec((1, SGU_W), lambda i: (0, 0)),
                  pl.BlockSpec((SGU_G, WIN, WIN), lambda i: (0, 0, 0)),
                  pl.BlockSpec((WIN, SGU_G), lambda i: (0, 0))],
        out_specs=pl.BlockSpec((t, SGU_W), lambda i: (i, 0)),
        out_shape=jax.ShapeDtypeStruct((n, SGU_W), jnp.bfloat16),
        compiler_params=_cp(("arbitrary",)),
        name="sgu",
    )(z, z, ln_g.reshape(1, SGU_W), ln_b.reshape(1, SGU_W), w_s, b_s.T)


def _top2_of4(a, b, c, d):
    m1, n1 = jnp.maximum(a, b), jnp.minimum(a, b)
    m2, n2 = jnp.maximum(c, d), jnp.minimum(c, d)
    top = jnp.maximum(m1, m2)
    second = jnp.maximum(jnp.minimum(m1, m2), jnp.maximum(n1, n2))
    return top + second


def _route(logits):
    rows = [logits[e:e + 1, :] for e in range(N_EXP)]
    mx = functools.reduce(jnp.maximum, rows)
    ex = [jnp.exp(r - mx) for r in rows]
    den = functools.reduce(lambda a, b: a + b, ex)
    p = [e / den for e in ex]
    score = [_top2_of4(*p[EPG * g:EPG * (g + 1)]) for g in range(N_GRP)]
    best, sel = score[0], jnp.zeros_like(score[0], dtype=jnp.int32)
    for g in range(1, N_GRP):
        better = score[g] > best
        best = jnp.where(better, score[g], best)
        sel = jnp.where(better, g, sel)
    cand = []
    for k in range(EPG):
        val = p[k]
        for g in range(1, N_GRP):
            val = jnp.where(sel == g, p[EPG * g + k], val)
        cand.append(val)
    p1, i1 = cand[0], jnp.zeros_like(sel)
    for k in range(1, EPG):
        better = cand[k] > p1
        p1 = jnp.where(better, cand[k], p1)
        i1 = jnp.where(better, k, i1)
    p2, i2 = jnp.full_like(p1, -1.0), jnp.zeros_like(sel)
    for k in range(EPG):
        better = (cand[k] > p2) & (i1 != k)
        p2 = jnp.where(better, cand[k], p2)
        i2 = jnp.where(better, k, i2)
    tot = p1 + p2
    ids = jnp.concatenate([sel * EPG + i1, sel * EPG + i2], axis=0)
    gates = jnp.concatenate([p1 / tot, p2 / tot], axis=0)
    return ids, gates


def _outproj_body(ya_ref, ys_ref, w_ref, h_ref, g_ref, b_ref, rwh_ref, rwl_ref, rb_ref,
                  hf_ref, hp_ref, id_ref, gt_ref):
    mix = jnp.dot(ya_ref[...], w_ref[:ATT_W, :], preferred_element_type=jnp.float32)
    mix = mix + jnp.dot(ys_ref[...], w_ref[ATT_W:, :], preferred_element_type=jnp.float32)
    h1 = _ln(ALPHA * h_ref[...] + mix, g_ref[...], b_ref[...])
    hf_ref[...] = h1
    _pack_rows(h1, hp_ref)
    hh = h1.astype(jnp.bfloat16)
    hl = (h1 - hh.astype(jnp.float32)).astype(jnp.bfloat16)
    nt = (((1,), (1,)), ((), ()))
    logits = lax.dot_general(rwh_ref[...], hh, nt, preferred_element_type=jnp.float32)
    logits = logits + lax.dot_general(rwh_ref[...], hl, nt, preferred_element_type=jnp.float32)
    logits = logits + lax.dot_general(rwl_ref[...], hh, nt, preferred_element_type=jnp.float32)
    ids, gates = _route(logits + rb_ref[...])
    id_ref[...] = ids
    gt_ref[...] = gates


def _outproj(ya, ys, wb, hf, g, b, rwh, rwl, rb):
    n = ya.shape[0]
    t = 256
    return pl.pallas_call(
        _outproj_body,
        grid=(n // t,),
        in_specs=[pl.BlockSpec((t, ATT_W), lambda i: (i, 0)),
                  pl.BlockSpec((t, SGU_W), lambda i: (i, 0)),
                  pl.BlockSpec((D, D), lambda i: (0, 0)),
                  pl.BlockSpec((t, D), lambda i: (i, 0)),
                  pl.BlockSpec((1, D), lambda i: (0, 0)),
                  pl.BlockSpec((1, D), lambda i: (0, 0)),
                  pl.BlockSpec((N_EXP, D), lambda i: (0, 0)),
                  pl.BlockSpec((N_EXP, D), lambda i: (0, 0)),
                  pl.BlockSpec((N_EXP, 1), lambda i: (0, 0))],
        out_specs=[pl.BlockSpec((t, D), lambda i: (i, 0)),
                   pl.BlockSpec((t * WPT, LANES), lambda i: (i, 0)),
                   pl.BlockSpec((2, t), lambda i: (0, i)),
                   pl.BlockSpec((2, t), lambda i: (0, i))],
        out_shape=[jax.ShapeDtypeStruct((n, D), jnp.float32),
                   jax.ShapeDtypeStruct((n * WPT, LANES), jnp.uint32),
                   jax.ShapeDtypeStruct((2, n), jnp.int32),
                   jax.ShapeDtypeStruct((2, n), jnp.float32)],
        compiler_params=_cp(("arbitrary",)),
        name="out_proj_ln_route",
    )(ya, ys, wb, hf, g.reshape(1, D), b.reshape(1, D), rwh, rwl, rb)


W_CHUNKS = 4
CAST_ROWS = 128


def _experts_body(te_ref, nt_ref, tend_ref, src_ref, dst_ref,
                  hp_ref, wg_ref, wu_ref, wd_ref, y_ref,
                  xa, xb, ya, yb, wgs, wus, wds, wgb, wub, wdb, sems, *, layer, n):
    step = pl.program_id(0)
    nt = nt_ref[0]
    gsem = (sems.at[0], sems.at[1])
    ssem = (sems.at[2], sems.at[3])
    wsem = sems.at[4]
    tile_rows = TM * WPT

    def weight_copies(e):
        copies = []
        for w_ref, stage in ((wg_ref, wgs), (wu_ref, wus), (wd_ref, wds)):
            rows = stage.shape[0] // W_CHUNKS
            for c in range(W_CHUNKS):
                copies.append(pltpu.make_async_copy(
                    w_ref.at[layer, e, pl.ds(c * rows, rows)],
                    stage.at[pl.ds(c * rows, rows)], wsem))
        return copies

    def gather_copy(row, r, xbuf, sem):
        tok = pl.multiple_of(src_ref[row] * WPT, WPT)
        return pltpu.make_async_copy(hp_ref.at[pl.ds(tok, WPT)],
                                     xbuf.at[pl.ds(r * WPT, WPT)], sem)

    def scatter_copy(entry, r, ybuf, sem):
        dst = pl.multiple_of(dst_ref[entry] * WPT, WPT)
        return pltpu.make_async_copy(ybuf.at[pl.ds(r * WPT, WPT)],
                                     y_ref.at[pl.ds(dst, WPT)], sem)

    def wait_gather(xbuf, sem):
        pltpu.make_async_copy(hp_ref.at[pl.ds(0, tile_rows)], xbuf, sem).wait()

    def wait_scatter(ybuf, sem):
        pltpu.make_async_copy(ybuf, y_ref.at[pl.ds(0, tile_rows)], sem).wait()

    @pl.when(step == 0)
    def _():
        for cp in weight_copies(te_ref[0]):
            cp.start()

        def issue(r, carry):
            gather_copy(r, r, xa, gsem[0]).start()
            return carry

        lax.fori_loop(0, TM, issue, 0)
        ya[...] = jnp.zeros_like(ya)
        yb[...] = jnp.zeros_like(yb)
        pltpu.make_async_copy(ya, y_ref.at[pl.ds(2 * n * WPT, tile_rows)], ssem[0]).start()

    def phase(t, xcur, gcur, xnext, gnext, ycur, scur, yprev, sprev):
        @pl.when(t < nt)
        def _():
            e = te_ref[t]

            @pl.when((t == 0) | (e != te_ref[jnp.maximum(t - 1, 0)]))
            def _():
                for cp in weight_copies(e):
                    cp.wait()
                for stage, work in ((wgs, wgb), (wus, wub), (wds, wdb)):
                    def cast(c, carry, stage=stage, work=work):
                        rows = pl.ds(pl.multiple_of(c * CAST_ROWS, CAST_ROWS), CAST_ROWS)
                        work[rows, :] = stage[rows, :].astype(jnp.bfloat16)
                        return carry

                    lax.fori_loop(0, stage.shape[0] // CAST_ROWS, cast, 0)
                nxt = tend_ref[e]

                @pl.when(nxt < nt)
                def _():
                    for cp in weight_copies(te_ref[jnp.minimum(nxt, te_ref.shape[0] - 1)]):
                        cp.start()

            wait_gather(xcur, gcur)
            tnext = jnp.minimum(t + 1, nt - 1)
            for r in range(TM):
                gather_copy(tnext * TM + r, r, xnext, gnext).start()
            for r in range(TM):
                scatter_copy(t * TM + r, r, yprev, sprev).start()
            x = jnp.concatenate([c.astype(jnp.bfloat16) for c in _unpack_rows(xcur, TM)],
                                axis=1)
            g = jnp.dot(x, wgb[...], preferred_element_type=jnp.float32)
            u = jnp.dot(x, wub[...], preferred_element_type=jnp.float32)
            hid = (g * jax.nn.sigmoid(g) * u).astype(jnp.bfloat16)
            y = jnp.dot(hid, wdb[...], preferred_element_type=jnp.float32)
            wait_scatter(ycur, scur)
            _pack_rows(y, ycur)

            @pl.when(t == nt - 1)
            def _():
                wait_gather(xnext, gnext)
                wait_scatter(yprev, sprev)

                def issue(r, carry):
                    scatter_copy((t + 1) * TM + r, r, ycur, scur).start()
                    return carry

                lax.fori_loop(0, TM, issue, 0)
                wait_scatter(ycur, scur)

    phase(2 * step, xa, gsem[0], xb, gsem[1], ya, ssem[0], yb, ssem[1])
    phase(2 * step + 1, xb, gsem[1], xa, gsem[0], yb, ssem[1], ya, ssem[0])


def _experts(tile_expert, ntiles, tile_end, src_tok, dst_ext, hp, w_gate, w_up, w_down,
             layer, n, max_tiles):
    any_spec = pl.BlockSpec(memory_space=pl.ANY)
    slab = pltpu.VMEM((TM * WPT, LANES), jnp.uint32)
    return pl.pallas_call(
        functools.partial(_experts_body, layer=layer, n=n),
        grid_spec=pltpu.PrefetchScalarGridSpec(
            num_scalar_prefetch=5,
            grid=(max_tiles // 2,),
            in_specs=[any_spec, any_spec, any_spec, any_spec],
            out_specs=any_spec,
            scratch_shapes=[slab, slab, slab, slab,
                            pltpu.VMEM((D, F), jnp.float32),
                            pltpu.VMEM((D, F), jnp.float32),
                            pltpu.VMEM((F, D), jnp.float32),
                            pltpu.VMEM((D, F), jnp.bfloat16),
                            pltpu.VMEM((D, F), jnp.bfloat16),
                            pltpu.VMEM((F, D), jnp.bfloat16),
                            pltpu.SemaphoreType.DMA((5,))]),
        out_shape=jax.ShapeDtypeStruct(((2 * n + 2 * TM) * WPT, LANES), jnp.uint32),
        compiler_params=pltpu.CompilerParams(dimension_semantics=("arbitrary",),
                                             vmem_limit_bytes=VMEM_LIMIT,
                                             disable_bounds_checks=True),
        name="experts",
    )(tile_expert, ntiles, tile_end, src_tok, dst_ext, hp, w_gate, w_up, w_down)


def _combine_body(y0_ref, y1_ref, h_ref, gt_ref, g_ref, b_ref, hf_ref, hb_ref):
    t = h_ref.shape[0]
    y0 = jnp.concatenate(_unpack_rows(y0_ref, t), axis=1)
    y1 = jnp.concatenate(_unpack_rows(y1_ref, t), axis=1)
    gt = gt_ref[...]
    ffn = gt[:, 0:1] * y0 + gt[:, 1:2] * y1
    h2 = _ln(ALPHA * h_ref[...] + ffn, g_ref[...], b_ref[...])
    hf_ref[...] = h2
    hb_ref[...] = h2.astype(jnp.bfloat16)


def _combine(yg, hf, gates_t, g, b):
    n = hf.shape[0]
    t = 256
    nb = n // t
    return pl.pallas_call(
        _combine_body,
        grid=(nb,),
        in_specs=[pl.BlockSpec((t * WPT, LANES), lambda i: (i, 0)),
                  pl.BlockSpec((t * WPT, LANES), lambda i: (nb + i, 0)),
                  pl.BlockSpec((t, D), lambda i: (i, 0)),
                  pl.BlockSpec((t, 2), lambda i: (i, 0)),
                  pl.BlockSpec((1, D), lambda i: (0, 0)),
                  pl.BlockSpec((1, D), lambda i: (0, 0))],
        out_specs=[pl.BlockSpec((t, D), lambda i: (i, 0)),
                   pl.BlockSpec((t, D), lambda i: (i, 0))],
        out_shape=[jax.ShapeDtypeStruct((n, D), jnp.float32),
                   jax.ShapeDtypeStruct((n, D), jnp.bfloat16)],
        compiler_params=_cp(("arbitrary",)),
        name="combine_ln",
    )(yg, yg, hf, gates_t, g.reshape(1, D), b.reshape(1, D))


def _dispatch_plan(ids, n, max_tiles):
    i32 = jnp.int32
    rows = max_tiles * TM
    e_flat = ids.reshape(-1)
    onehot = (e_flat[:, None] == jnp.arange(N_EXP, dtype=i32)[None, :]).astype(i32)
    csum = jnp.cumsum(onehot, axis=0)
    counts = csum[-1]
    tiles = (counts + TM - 1) // TM
    tile_end = jnp.cumsum(tiles).astype(i32)
    tile_start = tile_end - tiles
    pos = jnp.sum(onehot * (csum - 1 + (tile_start * TM)[None, :]), axis=1)
    asg = jnp.full((rows,), -1, i32).at[pos].set(jnp.arange(2 * n, dtype=i32))
    r = jnp.arange(rows, dtype=i32)
    valid = asg >= 0
    src_tok = jnp.where(valid, jnp.where(asg >= n, asg - n, asg), 0)
    dump = 2 * n + ((r // TM) % 2) * TM + (r % TM)
    dst_row = jnp.where(valid, asg, dump)
    dst_ext = jnp.concatenate([2 * n + TM + jnp.arange(TM, dtype=i32), dst_row])
    tile_ids = jnp.arange(max_tiles, dtype=i32)
    tile_expert = jnp.minimum(
        jnp.sum((tile_ids[:, None] >= tile_end[None, :]).astype(i32), axis=1), N_EXP - 1)
    return src_tok, dst_ext, tile_expert, tile_end[-1:], tile_end


def kernel(x, ln_in_g, ln_in_b, w_in, rel_bias, sgu_ln_g, sgu_ln_b, sgu_w, sgu_b, w_out,
           ln1_g, ln1_b, router_w, router_b, w_gate, w_up, w_down, ln2_g, ln2_b):
    batch, seq, _ = x.shape
    n = batch * seq
    max_tiles = (2 * n) // TM + N_EXP
    rwt = router_w.T
    rwh = rwt.astype(jnp.bfloat16)
    rwl = (rwt - rwh.astype(jnp.float32)).astype(jnp.bfloat16)
    rb = router_b.reshape(N_EXP, 1)

    hf, hb = _ln0(x.reshape(n, D), ln_in_g, ln_in_b)
    for l in range(DEPTH):
        z = _inproj(hb, w_in, l)
        ya = _attention(z, _attn_bias(rel_bias[l]), batch, seq)
        ys = _sgu(z, sgu_ln_g[l], sgu_ln_b[l], sgu_w[l], sgu_b[l])
        hf, hp, ids, gates = _outproj(ya, ys, w_out[l].astype(jnp.bfloat16), hf,
                                      ln1_g[l], ln1_b[l], rwh, rwl, rb)
        src_tok, dst_ext, tile_expert, ntiles, tile_end = _dispatch_plan(ids, n, max_tiles)
        yslots = _experts(tile_expert, ntiles, tile_end, src_tok, dst_ext, hp,
                          w_gate, w_up, w_down, l, n, max_tiles)
        hf, hb = _combine(yslots, hf, gates.T, ln2_g[l], ln2_b[l])
    return hf.reshape(batch, seq, D)
```

```python
import functools
import math

import jax
import jax.numpy as jnp
import numpy as np
from jax import lax
from jax.experimental import pallas as pl
from jax.experimental.pallas import tpu as pltpu

D = 2048
HEADS = 8
DH = 128
ATT_W = HEADS * DH
SGU_G = 8
SGU_W = SGU_G * 128
WIN = 128
IN_W = 3 * ATT_W + 2 * SGU_W
CHUNK = 64
N_PREV = 8
REL_CLIP = 128
N_EXP = 16
N_GRP = 4
EPG = 4
F = 1024
DEPTH = 2
ALPHA = (2.0 * DEPTH) ** 0.25
EPS = 1e-5
NEG = -1e30

LANES = 128
SUBLANES = 8
VMEM_LIMIT = 56 * 1024 * 1024

QB = 256
KWIN = QB + N_PREV * CHUNK
TM = 512
WPT = D // (2 * LANES)
W_CHUNKS = 4
CAST_ROWS = 128


def _cp(sem, vmem=VMEM_LIMIT, **kw):
    return pltpu.CompilerParams(dimension_semantics=sem, vmem_limit_bytes=vmem, **kw)


def _ln(x, g, b):
    mu = jnp.mean(x, axis=-1, keepdims=True)
    xc = x - mu
    var = jnp.mean(xc * xc, axis=-1, keepdims=True)
    return xc * lax.rsqrt(var + EPS) * g + b


def _pack_rows(y, o_ref):
    t = y.shape[0]
    bits = pltpu.bitcast(y.astype(jnp.bfloat16).astype(jnp.float32), jnp.uint32)
    for s in range(WPT):
        lo = bits[:, 2 * LANES * s:2 * LANES * s + LANES]
        hi = bits[:, 2 * LANES * s + LANES:2 * LANES * (s + 1)]
        o_ref[pl.ds(s, t, stride=WPT), :] = (lo >> 16) | (hi & jnp.uint32(0xFFFF0000))


def _unpack_rows(p_ref, t):
    chunks = []
    for s in range(WPT):
        w = p_ref[pl.ds(s, t, stride=WPT), :]
        chunks.append(pltpu.bitcast(w << 16, jnp.float32))
        chunks.append(pltpu.bitcast(w & jnp.uint32(0xFFFF0000), jnp.float32))
    return chunks


def _ln0_body(x_ref, g_ref, b_ref, hf_ref, hb_ref):
    y = _ln(x_ref[...], g_ref[...], b_ref[...])
    hf_ref[...] = y
    hb_ref[...] = y.astype(jnp.bfloat16)


def _ln0(x, g, b):
    n = x.shape[0]
    t = 256
    return pl.pallas_call(
        _ln0_body,
        grid=(n // t,),
        in_specs=[pl.BlockSpec((t, D), lambda i: (i, 0)),
                  pl.BlockSpec((1, D), lambda i: (0, 0)),
                  pl.BlockSpec((1, D), lambda i: (0, 0))],
        out_specs=[pl.BlockSpec((t, D), lambda i: (i, 0)),
                   pl.BlockSpec((t, D), lambda i: (i, 0))],
        out_shape=[jax.ShapeDtypeStruct((n, D), jnp.float32),
                   jax.ShapeDtypeStruct((n, D), jnp.bfloat16)],
        compiler_params=_cp(("arbitrary",)),
        name="ln_in",
    )(x, g.reshape(1, D), b.reshape(1, D))


def _inproj_body(x_ref, w_ref, o_ref, wb_ref):
    @pl.when(pl.program_id(1) == 0)
    def _():
        wb_ref[...] = w_ref[0].astype(jnp.bfloat16)

    o_ref[...] = jnp.dot(x_ref[...], wb_ref[...],
                         preferred_element_type=jnp.float32).astype(o_ref.dtype)


def _inproj(hb, w, layer):
    n = hb.shape[0]
    tm, tn = 2048, 512
    return pl.pallas_call(
        _inproj_body,
        grid=(IN_W // tn, n // tm),
        in_specs=[pl.BlockSpec((tm, D), lambda j, i: (i, 0)),
                  pl.BlockSpec((1, D, tn), lambda j, i: (layer, 0, j))],
        out_specs=pl.BlockSpec((tm, tn), lambda j, i: (i, j)),
        out_shape=jax.ShapeDtypeStruct((n, IN_W), jnp.bfloat16),
        scratch_shapes=[pltpu.VMEM((D, tn), jnp.bfloat16)],
        compiler_params=_cp(("arbitrary", "arbitrary")),
        name="in_proj",
    )(hb, w)


def _attn_body(q_ref, k_ref, v_ref, bm_ref, o_ref):
    seq = q_ref.shape[0]
    scale = 1.0 / math.sqrt(DH)
    for j in range(seq // QB):
        q0 = j * QB
        k0 = max(0, q0 - N_PREV * CHUNK)
        nk = q0 + QB - k0
        q = q_ref[q0:q0 + QB, :]
        k = k_ref[k0:k0 + nk, :]
        v = v_ref[k0:k0 + nk, :]
        s = lax.dot_general(q, k, (((1,), (1,)), ((), ())),
                            preferred_element_type=jnp.float32)
        s = s * scale + bm_ref[0, :, KWIN - nk:]
        m = jnp.max(s, axis=-1, keepdims=True)
        p = jnp.exp(s - m)
        l = jnp.sum(p, axis=-1, keepdims=True)
        o = jnp.dot(p.astype(jnp.bfloat16), v, preferred_element_type=jnp.float32)
        o_ref[q0:q0 + QB, :] = (o / l).astype(o_ref.dtype)


def _attn_bias(rel_table):
    period = KWIN + QB
    off = N_PREV * CHUNK
    tab = rel_table.astype(jnp.float32)
    heads = tab.shape[0]
    far = tab[:, 2 * REL_CLIP:]
    n_far = off - REL_CLIP + 1
    near = tab[:, 1:2 * REL_CLIP][:, ::-1]
    n_low = KWIN - n_far - near.shape[1]
    g = jnp.concatenate([jnp.broadcast_to(far, (heads, n_far)), near,
                         jnp.broadcast_to(tab[:, :1], (heads, n_low)),
                         jnp.broadcast_to(far, (heads, period - KWIN))], axis=1)
    bias = jnp.tile(g, (1, QB + 1))[:, :QB * (period - 1)].reshape(heads, QB, period - 1)
    bias = bias[:, :, :KWIN]
    i = np.arange(QB)[:, None]
    c = np.arange(KWIN)[None, :]
    qc = i // CHUNK
    kc = c // CHUNK
    band = (kc >= qc) & (kc <= qc + N_PREV)
    return jnp.where(jnp.asarray(band)[None], bias, NEG)


def _attention(z, bm, batch, seq):
    return pl.pallas_call(
        _attn_body,
        grid=(batch, HEADS),
        in_specs=[pl.BlockSpec((seq, DH), lambda b, h: (b, h)),
                  pl.BlockSpec((seq, DH), lambda b, h: (b, HEADS + h)),
                  pl.BlockSpec((seq, DH), lambda b, h: (b, 2 * HEADS + h)),
                  pl.BlockSpec((1, QB, KWIN), lambda b, h: (h, 0, 0))],
        out_specs=pl.BlockSpec((seq, DH), lambda b, h: (b, h)),
        out_shape=jax.ShapeDtypeStruct((batch * seq, ATT_W), jnp.bfloat16),
        compiler_params=_cp(("arbitrary", "arbitrary")),
        name="band_attn",
    )(z, z, z, bm)


def _gelu(x):
    return 0.5 * x * (1.0 + lax.erf(x * (1.0 / math.sqrt(2.0))))


def _sgu_body(u_ref, v_ref, g_ref, b_ref, w_ref, bs_ref, o_ref):
    t = u_ref.shape[0]
    u = _gelu(u_ref[...].astype(jnp.float32))
    v = _gelu(v_ref[...].astype(jnp.float32))
    vn = _ln(v, g_ref[...], b_ref[...]).astype(jnp.bfloat16)
    row = lax.broadcasted_iota(jnp.int32, (WIN, WIN), 0)
    col = lax.broadcasted_iota(jnp.int32, (WIN, WIN), 1)
    causal = row >= col
    for g in range(SGU_G):
        wm = jnp.where(causal, w_ref[g], 0.0).astype(jnp.bfloat16)
        bias = bs_ref[:, g:g + 1]
        for n in range(t // WIN):
            r0, c0 = n * WIN, g * 128
            mixed = jnp.dot(wm, vn[r0:r0 + WIN, c0:c0 + 128],
                            preferred_element_type=jnp.float32) + bias
            o_ref[r0:r0 + WIN, c0:c0 + 128] = (u[r0:r0 + WIN, c0:c0 + 128] * mixed
                                               ).astype(o_ref.dtype)


def _sgu(z, ln_g, ln_b, w_s, b_s):
    n = z.shape[0]
    t = 256
    ublk = 3 * ATT_W // SGU_W
    return pl.pallas_call(
        _sgu_body,
        grid=(n // t,),
        in_specs=[pl.BlockSpec((t, SGU_W), lambda i: (i, ublk)),
                  pl.BlockSpec((t, SGU_W), lambda i: (i, ublk + 1)),
                  pl.BlockSpec((1, SGU_W), lambda i: (0, 0)),
                  pl.BlockSpec((1, SGU_W), lambda i: (0, 0)),
                  pl.BlockSpec((SGU_G, WIN, WIN), lambda i: (0, 0, 0)),
                  pl.BlockSpec((WIN, SGU_G), lambda i: (0, 0))],
        out_specs=pl.BlockSpec((t, SGU_W), lambda i: (i, 0)),
        out_shape=jax.ShapeDtypeStruct((n, SGU_W), jnp.bfloat16),
        compiler_params=_cp(("arbitrary",)),
        name="sgu",
    )(z, z, ln_g.reshape(1, SGU_W), ln_b.reshape(1, SGU_W), w_s, b_s.T)


def _top2_of4(a, b, c, d):
    m1, n1 = jnp.maximum(a, b), jnp.minimum(a, b)
    m2, n2 = jnp.maximum(c, d), jnp.minimum(c, d)
    top = jnp.maximum(m1, m2)
    second = jnp.maximum(jnp.minimum(m1, m2), jnp.maximum(n1, n2))
    return top + second


def _route(logits):
    rows = [logits[e:e + 1, :] for e in range(N_EXP)]
    mx = functools.reduce(jnp.maximum, rows)
    ex = [jnp.exp(r - mx) for r in rows]
    den = functools.reduce(lambda a, b: a + b, ex)
    p = [e / den for e in ex]
    score = [_top2_of4(*p[EPG * g:EPG * (g + 1)]) for g in range(N_GRP)]
    best, sel = score[0], jnp.zeros_like(score[0], dtype=jnp.int32)
    for g in range(1, N_GRP):
        better = score[g] > best
        best = jnp.where(better, score[g], best)
        sel = jnp.where(better, g, sel)
    cand = []
    for k in range(EPG):
        val = p[k]
        for g in range(1, N_GRP):
            val = jnp.where(sel == g, p[EPG * g + k], val)
        cand.append(val)
    p1, i1 = cand[0], jnp.zeros_like(sel)
    for k in range(1, EPG):
        better = cand[k] > p1
        p1 = jnp.where(better, cand[k], p1)
        i1 = jnp.where(better, k, i1)
    p2, i2 = jnp.full_like(p1, -1.0), jnp.zeros_like(sel)
    for k in range(EPG):
        better = (cand[k] > p2) & (i1 != k)
        p2 = jnp.where(better, cand[k], p2)
        i2 = jnp.where(better, k, i2)
    tot = p1 + p2
    ids = jnp.concatenate([sel * EPG + i1, sel * EPG + i2], axis=0)
    gates = jnp.concatenate([p1 / tot, p2 / tot], axis=0)
    return ids, gates


def _outproj_body(ya_ref, ys_ref, w_ref, h_ref, g_ref, b_ref, rwh_ref, rwl_ref, rb_ref,
                  hf_ref, hp_ref, id_ref, gt_ref):
    mix = jnp.dot(ya_ref[...], w_ref[:ATT_W, :], preferred_element_type=jnp.float32)
    mix = mix + jnp.dot(ys_ref[...], w_ref[ATT_W:, :], preferred_element_type=jnp.float32)
    h1 = _ln(ALPHA * h_ref[...] + mix, g_ref[...], b_ref[...])
    hf_ref[...] = h1
    _pack_rows(h1, hp_ref)
    hh = h1.astype(jnp.bfloat16)
    hl = (h1 - hh.astype(jnp.float32)).astype(jnp.bfloat16)
    nt = (((1,), (1,)), ((), ()))
    logits = lax.dot_general(rwh_ref[...], hh, nt, preferred_element_type=jnp.float32)
    logits = logits + lax.dot_general(rwh_ref[...], hl, nt, preferred_element_type=jnp.float32)
    logits = logits + lax.dot_general(rwl_ref[...], hh, nt, preferred_element_type=jnp.float32)
    ids, gates = _route(logits + rb_ref[...])
    id_ref[...] = ids
    gt_ref[...] = gates


def _outproj(ya, ys, wb, hf, g, b, rwh, rwl, rb):
    n = ya.shape[0]
    t = 512
    return pl.pallas_call(
        _outproj_body,
        grid=(n // t,),
        in_specs=[pl.BlockSpec((t, ATT_W), lambda i: (i, 0)),
                  pl.BlockSpec((t, SGU_W), lambda i: (i, 0)),
                  pl.BlockSpec((D, D), lambda i: (0, 0)),
                  pl.BlockSpec((t, D), lambda i: (i, 0)),
                  pl.BlockSpec((1, D), lambda i: (0, 0)),
                  pl.BlockSpec((1, D), lambda i: (0, 0)),
                  pl.BlockSpec((N_EXP, D), lambda i: (0, 0)),
                  pl.BlockSpec((N_EXP, D), lambda i: (0, 0)),
                  pl.BlockSpec((N_EXP, 1), lambda i: (0, 0))],
        out_specs=[pl.BlockSpec((t, D), lambda i: (i, 0)),
                   pl.BlockSpec((t * WPT, LANES), lambda i: (i, 0)),
                   pl.BlockSpec((2, t), lambda i: (0, i)),
                   pl.BlockSpec((2, t), lambda i: (0, i))],
        out_shape=[jax.ShapeDtypeStruct((n, D), jnp.float32),
                   jax.ShapeDtypeStruct((n * WPT, LANES), jnp.uint32),
                   jax.ShapeDtypeStruct((2, n), jnp.int32),
                   jax.ShapeDtypeStruct((2, n), jnp.float32)],
        compiler_params=_cp(("arbitrary",)),
        name="out_proj_ln_route",
    )(ya, ys, wb, hf, g.reshape(1, D), b.reshape(1, D), rwh, rwl, rb)


def _weight_copies(pairs, layer, e, sem):
    copies = []
    for w_ref, stage, _ in pairs:
        rows = stage.shape[0] // W_CHUNKS
        for c in range(W_CHUNKS):
            copies.append(pltpu.make_async_copy(w_ref.at[layer, e, pl.ds(c * rows, rows)],
                                                stage.at[pl.ds(c * rows, rows)], sem))
    return copies


def _refresh_weights(pairs, layer, te_ref, tend_ref, nt, t, sem):
    e = te_ref[t]

    @pl.when((t == 0) | (e != te_ref[jnp.maximum(t - 1, 0)]))
    def _():
        for cp in _weight_copies(pairs, layer, e, sem):
            cp.wait()
        for _, stage, work in pairs:
            def cast(c, carry, stage=stage, work=work):
                rows = pl.ds(pl.multiple_of(c * CAST_ROWS, CAST_ROWS), CAST_ROWS)
                work[rows, :] = stage[rows, :].astype(jnp.bfloat16)
                return carry

            lax.fori_loop(0, stage.shape[0] // CAST_ROWS, cast, 0)
        nxt = tend_ref[e]

        @pl.when(nxt < nt)
        def _():
            e_next = te_ref[jnp.minimum(nxt, te_ref.shape[0] - 1)]
            for cp in _weight_copies(pairs, layer, e_next, sem):
                cp.start()


def _ffn_up_body(te_ref, nt_ref, tend_ref, src_ref, hp_ref, wg_ref, wu_ref, o_ref,
                 xa, xb, wgs, wus, wgb, wub, sems, *, layer):
    step = pl.program_id(0)
    nt = nt_ref[0]
    gsem = (sems.at[0], sems.at[1])
    wsem = sems.at[2]
    pairs = ((wg_ref, wgs, wgb), (wu_ref, wus, wub))

    def gather_copy(row, r, xbuf, sem):
        tok = pl.multiple_of(src_ref[row] * WPT, WPT)
        return pltpu.make_async_copy(hp_ref.at[pl.ds(tok, WPT)],
                                     xbuf.at[pl.ds(r * WPT, WPT)], sem)

    def wait_gather(xbuf, sem):
        pltpu.make_async_copy(hp_ref.at[pl.ds(0, TM * WPT)], xbuf, sem).wait()

    @pl.when(step == 0)
    def _():
        for cp in _weight_copies(pairs, layer, te_ref[0], wsem):
            cp.start()

        def issue(r, carry):
            gather_copy(r, r, xa, gsem[0]).start()
            return carry

        lax.fori_loop(0, TM, issue, 0)

    def phase(t, xcur, gcur, xnext, gnext, rows):
        @pl.when(t < nt)
        def _():
            _refresh_weights(pairs, layer, te_ref, tend_ref, nt, t, wsem)
            wait_gather(xcur, gcur)
            tnext = jnp.minimum(t + 1, nt - 1)
            for r in range(TM):
                gather_copy(tnext * TM + r, r, xnext, gnext).start()
            x = jnp.concatenate([c.astype(jnp.bfloat16) for c in _unpack_rows(xcur, TM)],
                                axis=1)
            g = jnp.dot(x, wgb[...], preferred_element_type=jnp.float32)
            u = jnp.dot(x, wub[...], preferred_element_type=jnp.float32)
            o_ref[rows, :] = (g * jax.nn.sigmoid(g) * u).astype(o_ref.dtype)

            @pl.when(t == nt - 1)
            def _():
                wait_gather(xnext, gnext)

        @pl.when(t >= nt)
        def _():
            o_ref[rows, :] = jnp.zeros((TM, F), o_ref.dtype)

    phase(2 * step, xa, gsem[0], xb, gsem[1], slice(0, TM))
    phase(2 * step + 1, xb, gsem[1], xa, gsem[0], slice(TM, 2 * TM))


def _ffn_down_body(te_ref, nt_ref, tend_ref, dst_ref, h_ref, wd_ref, y_ref,
                   ya, yb, wds, wdb, sems, *, layer, n):
    step = pl.program_id(0)
    nt = nt_ref[0]
    ssem = (sems.at[0], sems.at[1])
    wsem = sems.at[2]
    pairs = ((wd_ref, wds, wdb),)
    tile_rows = TM * WPT

    def scatter_copy(entry, r, ybuf, sem):
        dst = pl.multiple_of(dst_ref[entry] * WPT, WPT)
        return pltpu.make_async_copy(ybuf.at[pl.ds(r * WPT, WPT)],
                                     y_ref.at[pl.ds(dst, WPT)], sem)

    def wait_scatter(ybuf, sem):
        pltpu.make_async_copy(ybuf, y_ref.at[pl.ds(0, tile_rows)], sem).wait()

    @pl.when(step == 0)
    def _():
        for cp in _weight_copies(pairs, layer, te_ref[0], wsem):
            cp.start()
        ya[...] = jnp.zeros_like(ya)
        yb[...] = jnp.zeros_like(yb)
        pltpu.make_async_copy(ya, y_ref.at[pl.ds(2 * n * WPT, tile_rows)], ssem[0]).start()

    def phase(t, rows, ycur, scur, yprev, sprev):
        @pl.when(t < nt)
        def _():
            _refresh_weights(pairs, layer, te_ref, tend_ref, nt, t, wsem)
            for r in range(TM):
                scatter_copy(t * TM + r, r, yprev, sprev).start()
            y = jnp.dot(h_ref[rows, :], wdb[...], preferred_element_type=jnp.float32)
            wait_scatter(ycur, scur)
            _pack_rows(y, ycur)

            @pl.when(t == nt - 1)
            def _():
                wait_scatter(yprev, sprev)

                def issue(r, carry):
                    scatter_copy((t + 1) * TM + r, r, ycur, scur).start()
                    return carry

                lax.fori_loop(0, TM, issue, 0)
                wait_scatter(ycur, scur)

    phase(2 * step, slice(0, TM), ya, ssem[0], yb, ssem[1])
    phase(2 * step + 1, slice(TM, 2 * TM), yb, ssem[1], ya, ssem[0])


def _experts(tile_expert, ntiles, tile_end, src_tok, dst_ext, hp, w_gate, w_up, w_down,
             layer, n, max_tiles):
    any_spec = pl.BlockSpec(memory_space=pl.ANY)
    slab = pltpu.VMEM((TM * WPT, LANES), jnp.uint32)
    steps = max_tiles // 2
    params = _cp(("arbitrary",), disable_bounds_checks=True)
    hid = pl.pallas_call(
        functools.partial(_ffn_up_body, layer=layer),
        grid_spec=pltpu.PrefetchScalarGridSpec(
            num_scalar_prefetch=4,
            grid=(steps,),
            in_specs=[any_spec, any_spec, any_spec],
            out_specs=pl.BlockSpec((2 * TM, F), lambda i, *_: (i, 0)),
            scratch_shapes=[slab, slab,
                            pltpu.VMEM((D, F), jnp.float32),
                            pltpu.VMEM((D, F), jnp.float32),
                            pltpu.VMEM((D, F), jnp.bfloat16),
                            pltpu.VMEM((D, F), jnp.bfloat16),
                            pltpu.SemaphoreType.DMA((3,))]),
        out_shape=jax.ShapeDtypeStruct((max_tiles * TM, F), jnp.bfloat16),
        compiler_params=params,
        name="expert_up",
    )(tile_expert, ntiles, tile_end, src_tok, hp, w_gate, w_up)
    return pl.pallas_call(
        functools.partial(_ffn_down_body, layer=layer, n=n),
        grid_spec=pltpu.PrefetchScalarGridSpec(
            num_scalar_prefetch=4,
            grid=(steps,),
            in_specs=[pl.BlockSpec((2 * TM, F), lambda i, *_: (i, 0)), any_spec],
            out_specs=any_spec,
            scratch_shapes=[slab, slab,
                            pltpu.VMEM((F, D), jnp.float32),
                            pltpu.VMEM((F, D), jnp.bfloat16),
                            pltpu.SemaphoreType.DMA((3,))]),
        out_shape=jax.ShapeDtypeStruct(((2 * n + 2 * TM) * WPT, LANES), jnp.uint32),
        compiler_params=params,
        name="expert_down",
    )(tile_expert, ntiles, tile_end, dst_ext, hid, w_down)


def _combine_body(y0_ref, y1_ref, h_ref, gt_ref, g_ref, b_ref, hf_ref, hb_ref):
    t = h_ref.shape[0]
    y0 = jnp.concatenate(_unpack_rows(y0_ref, t), axis=1)
    y1 = jnp.concatenate(_unpack_rows(y1_ref, t), axis=1)
    gt = gt_ref[...]
    ffn = gt[:, 0:1] * y0 + gt[:, 1:2] * y1
    h2 = _ln(ALPHA * h_ref[...] + ffn, g_ref[...], b_ref[...])
    hf_ref[...] = h2
    hb_ref[...] = h2.astype(jnp.bfloat16)


def _combine(yg, hf, gates_t, g, b):
    n = hf.shape[0]
    t = 256
    nb = n // t
    return pl.pallas_call(
        _combine_body,
        grid=(nb,),
        in_specs=[pl.BlockSpec((t * WPT, LANES), lambda i: (i, 0)),
                  pl.BlockSpec((t * WPT, LANES), lambda i: (nb + i, 0)),
                  pl.BlockSpec((t, D), lambda i: (i, 0)),
                  pl.BlockSpec((t, 2), lambda i: (i, 0)),
                  pl.BlockSpec((1, D), lambda i: (0, 0)),
                  pl.BlockSpec((1, D), lambda i: (0, 0))],
        out_specs=[pl.BlockSpec((t, D), lambda i: (i, 0)),
                   pl.BlockSpec((t, D), lambda i: (i, 0))],
        out_shape=[jax.ShapeDtypeStruct((n, D), jnp.float32),
                   jax.ShapeDtypeStruct((n, D), jnp.bfloat16)],
        compiler_params=_cp(("arbitrary",)),
        name="combine_ln",
    )(yg, yg, hf, gates_t, g.reshape(1, D), b.reshape(1, D))


def _dispatch_plan(ids, n, max_tiles):
    i32 = jnp.int32
    rows = max_tiles * TM
    e_flat = ids.reshape(-1)
    onehot = (e_flat[:, None] == jnp.arange(N_EXP, dtype=i32)[None, :]).astype(i32)
    csum = jnp.cumsum(onehot, axis=0)
    counts = csum[-1]
    tiles = (counts + TM - 1) // TM
    tile_end = jnp.cumsum(tiles).astype(i32)
    tile_start = tile_end - tiles
    pos = jnp.sum(onehot * (csum - 1 + (tile_start * TM)[None, :]), axis=1)
    asg = jnp.full((rows,), -1, i32).at[pos].set(jnp.arange(2 * n, dtype=i32))
    r = jnp.arange(rows, dtype=i32)
    valid = asg >= 0
    src_tok = jnp.where(valid, jnp.where(asg >= n, asg - n, asg), 0)
    dump = 2 * n + ((r // TM) % 2) * TM + (r % TM)
    dst_row = jnp.where(valid, asg, dump)
    dst_ext = jnp.concatenate([2 * n + TM + jnp.arange(TM, dtype=i32), dst_row])
    tile_ids = jnp.arange(max_tiles, dtype=i32)
    tile_expert = jnp.minimum(
        jnp.sum((tile_ids[:, None] >= tile_end[None, :]).astype(i32), axis=1), N_EXP - 1)
    return src_tok, dst_ext, tile_expert, tile_end[-1:], tile_end


def kernel(x, ln_in_g, ln_in_b, w_in, rel_bias, sgu_ln_g, sgu_ln_b, sgu_w, sgu_b, w_out,
           ln1_g, ln1_b, router_w, router_b, w_gate, w_up, w_down, ln2_g, ln2_b):
    batch, seq, _ = x.shape
    n = batch * seq
    max_tiles = (2 * n) // TM + N_EXP
    rwt = router_w.T
    rwh = rwt.astype(jnp.bfloat16)
    rwl = (rwt - rwh.astype(jnp.float32)).astype(jnp.bfloat16)
    rb = router_b.reshape(N_EXP, 1)

    hf, hb = _ln0(x.reshape(n, D), ln_in_g, ln_in_b)
    for l in range(DEPTH):
        z = _inproj(hb, w_in, l)
        ya = _attention(z, _attn_bias(rel_bias[l]), batch, seq)
        ys = _sgu(z, sgu_ln_g[l], sgu_ln_b[l], sgu_w[l], sgu_b[l])
        hf, hp, ids, gates = _outproj(ya, ys, w_out[l].astype(jnp.bfloat16), hf,
                                      ln1_g[l], ln1_b[l], rwh, rwl, rb)
        src_tok, dst_ext, tile_expert, ntiles, tile_end = _dispatch_plan(ids, n, max_tiles)
        yslots = _experts(tile_expert, ntiles, tile_end, src_tok, dst_ext, hp,
                          w_gate, w_up, w_down, l, n, max_tiles)
        hf, hb = _combine(yslots, hf, gates.T, ln2_g[l], ln2_b[l])
    return hf.reshape(batch, seq, D)
```

```python
import functools
import math

import jax
import jax.numpy as jnp
import numpy as np
from jax import lax
from jax.experimental import pallas as pl
from jax.experimental.pallas import tpu as pltpu

D = 2048
HEADS = 8
DH = 128
ATT_W = HEADS * DH
SGU_G = 8
SGU_W = SGU_G * 128
WIN = 128
IN_W = 3 * ATT_W + 2 * SGU_W
CHUNK = 64
N_PREV = 8
REL_CLIP = 128
N_EXP = 16
N_GRP = 4
EPG = 4
F = 1024
DEPTH = 2
ALPHA = (2.0 * DEPTH) ** 0.25
EPS = 1e-5
NEG = -1e30

LANES = 128
SUBLANES = 8
VMEM_LIMIT = 56 * 1024 * 1024

QB = 256
KWIN = QB + N_PREV * CHUNK
TM = 256
WPT = D // LANES
W_CHUNKS = 4
CAST_ROWS = 128
PITCH = 17
W_PRIORITY = 1


def _cp(sem, vmem=VMEM_LIMIT, **kw):
    return pltpu.CompilerParams(dimension_semantics=sem, vmem_limit_bytes=vmem, **kw)


def _ln(x, g, b):
    mu = jnp.mean(x, axis=-1, keepdims=True)
    xc = x - mu
    var = jnp.mean(xc * xc, axis=-1, keepdims=True)
    return xc * lax.rsqrt(var + EPS) * g + b


def _pack_rows(y, o_ref, pitch=WPT):
    t = y.shape[0]
    for s in range(WPT):
        o_ref[pl.ds(s, t, stride=pitch), :] = y[:, LANES * s:LANES * (s + 1)]


def _unpack_rows(p_ref, t, pitch=WPT):
    return [p_ref[pl.ds(s, t, stride=pitch), :] for s in range(WPT)]


def _ln0_body(x_ref, g_ref, b_ref, hf_ref, hb_ref):
    y = _ln(x_ref[...], g_ref[...], b_ref[...])
    hf_ref[...] = y
    hb_ref[...] = y.astype(jnp.bfloat16)


def _ln0(x, g, b):
    n = x.shape[0]
    t = 256
    return pl.pallas_call(
        _ln0_body,
        grid=(n // t,),
        in_specs=[pl.BlockSpec((t, D), lambda i: (i, 0)),
                  pl.BlockSpec((1, D), lambda i: (0, 0)),
                  pl.BlockSpec((1, D), lambda i: (0, 0))],
        out_specs=[pl.BlockSpec((t, D), lambda i: (i, 0)),
                   pl.BlockSpec((t, D), lambda i: (i, 0))],
        out_shape=[jax.ShapeDtypeStruct((n, D), jnp.float32),
                   jax.ShapeDtypeStruct((n, D), jnp.bfloat16)],
        compiler_params=_cp(("arbitrary",)),
        name="ln_in",
    )(x, g.reshape(1, D), b.reshape(1, D))


def _inproj_body(x_ref, w_ref, o_ref, wb_ref):
    @pl.when(pl.program_id(1) == 0)
    def _():
        wb_ref[...] = w_ref[0].astype(jnp.bfloat16)

    o_ref[...] = jnp.dot(x_ref[...], wb_ref[...],
                         preferred_element_type=jnp.float32).astype(o_ref.dtype)


def _inproj(hb, w, layer):
    n = hb.shape[0]
    tm, tn = 1024, 1024
    return pl.pallas_call(
        _inproj_body,
        grid=(IN_W // tn, n // tm),
        in_specs=[pl.BlockSpec((tm, D), lambda j, i: (i, 0)),
                  pl.BlockSpec((1, D, tn), lambda j, i: (layer, 0, j))],
        out_specs=pl.BlockSpec((tm, tn), lambda j, i: (i, j)),
        out_shape=jax.ShapeDtypeStruct((n, IN_W), jnp.bfloat16),
        scratch_shapes=[pltpu.VMEM((D, tn), jnp.bfloat16)],
        compiler_params=_cp(("arbitrary", "arbitrary")),
        name="in_proj",
    )(hb, w)


def _attn_body(q_ref, k_ref, v_ref, bm_ref, o_ref):
    seq = q_ref.shape[0]
    scale = 1.0 / math.sqrt(DH)
    for j in range(seq // QB):
        q0 = j * QB
        k0 = max(0, q0 - N_PREV * CHUNK)
        nk = q0 + QB - k0
        q = q_ref[q0:q0 + QB, :]
        k = k_ref[k0:k0 + nk, :]
        v = v_ref[k0:k0 + nk, :]
        s = lax.dot_general(q, k, (((1,), (1,)), ((), ())),
                            preferred_element_type=jnp.float32)
        s = s * scale + bm_ref[0, :, KWIN - nk:]
        m = jnp.max(s, axis=-1, keepdims=True)
        p = jnp.exp(s - m)
        l = jnp.sum(p, axis=-1, keepdims=True)
        o = jnp.dot(p.astype(jnp.bfloat16), v, preferred_element_type=jnp.float32)
        o_ref[q0:q0 + QB, :] = (o / l).astype(o_ref.dtype)


def _attn_bias(rel_table):
    period = KWIN + QB
    off = N_PREV * CHUNK
    tab = rel_table.astype(jnp.float32)
    heads = tab.shape[0]
    far = tab[:, 2 * REL_CLIP:]
    n_far = off - REL_CLIP + 1
    near = tab[:, 1:2 * REL_CLIP][:, ::-1]
    n_low = KWIN - n_far - near.shape[1]
    g = jnp.concatenate([jnp.broadcast_to(far, (heads, n_far)), near,
                         jnp.broadcast_to(tab[:, :1], (heads, n_low)),
                         jnp.broadcast_to(far, (heads, period - KWIN))], axis=1)
    bias = jnp.tile(g, (1, QB + 1))[:, :QB * (period - 1)].reshape(heads, QB, period - 1)
    bias = bias[:, :, :KWIN]
    i = np.arange(QB)[:, None]
    c = np.arange(KWIN)[None, :]
    qc = i // CHUNK
    kc = c // CHUNK
    band = (kc >= qc) & (kc <= qc + N_PREV)
    return jnp.where(jnp.asarray(band)[None], bias, NEG)


def _attention(z, bm, batch, seq):
    return pl.pallas_call(
        _attn_body,
        grid=(batch, HEADS),
        in_specs=[pl.BlockSpec((seq, DH), lambda b, h: (b, h)),
                  pl.BlockSpec((seq, DH), lambda b, h: (b, HEADS + h)),
                  pl.BlockSpec((seq, DH), lambda b, h: (b, 2 * HEADS + h)),
                  pl.BlockSpec((1, QB, KWIN), lambda b, h: (h, 0, 0))],
        out_specs=pl.BlockSpec((seq, DH), lambda b, h: (b, h)),
        out_shape=jax.ShapeDtypeStruct((batch * seq, ATT_W), jnp.bfloat16),
        compiler_params=_cp(("arbitrary", "arbitrary")),
        name="band_attn",
    )(z, z, z, bm)


def _gelu(x):
    return 0.5 * x * (1.0 + lax.erf(x * (1.0 / math.sqrt(2.0))))


def _sgu_body(u_ref, v_ref, g_ref, b_ref, w_ref, bs_ref, o_ref):
    t = u_ref.shape[0]
    u = _gelu(u_ref[...].astype(jnp.float32))
    v = _gelu(v_ref[...].astype(jnp.float32))
    vn = _ln(v, g_ref[...], b_ref[...]).astype(jnp.bfloat16)
    row = lax.broadcasted_iota(jnp.int32, (WIN, WIN), 0)
    col = lax.broadcasted_iota(jnp.int32, (WIN, WIN), 1)
    causal = row >= col
    for g in range(SGU_G):
        wm = jnp.where(causal, w_ref[g], 0.0).astype(jnp.bfloat16)
        bias = bs_ref[:, g:g + 1]
        for n in range(t // WIN):
            r0, c0 = n * WIN, g * 128
            mixed = jnp.dot(wm, vn[r0:r0 + WIN, c0:c0 + 128],
                            preferred_element_type=jnp.float32) + bias
            o_ref[r0:r0 + WIN, c0:c0 + 128] = (u[r0:r0 + WIN, c0:c0 + 128] * mixed
                                               ).astype(o_ref.dtype)


def _sgu(z, ln_g, ln_b, w_s, b_s):
    n = z.shape[0]
    t = 256
    ublk = 3 * ATT_W // SGU_W
    return pl.pallas_call(
        _sgu_body,
        grid=(n // t,),
        in_specs=[pl.BlockSpec((t, SGU_W), lambda i: (i, ublk)),
                  pl.BlockSpec((t, SGU_W), lambda i: (i, ublk + 1)),
                  pl.BlockSpec((1, SGU_W), lambda i: (0, 0)),
                  pl.BlockSpec((1, SGU_W), lambda i: (0, 0)),
                  pl.BlockSpec((SGU_G, WIN, WIN), lambda i: (0, 0, 0)),
                  pl.BlockSpec((WIN, SGU_G), lambda i: (0, 0))],
        out_specs=pl.BlockSpec((t, SGU_W), lambda i: (i, 0)),
        out_shape=jax.ShapeDtypeStruct((n, SGU_W), jnp.bfloat16),
        compiler_params=_cp(("arbitrary",)),
        name="sgu",
    )(z, z, ln_g.reshape(1, SGU_W), ln_b.reshape(1, SGU_W), w_s, b_s.T)


def _top2_of4(a, b, c, d):
    m1, n1 = jnp.maximum(a, b), jnp.minimum(a, b)
    m2, n2 = jnp.maximum(c, d), jnp.minimum(c, d)
    top = jnp.maximum(m1, m2)
    second = jnp.maximum(jnp.minimum(m1, m2), jnp.maximum(n1, n2))
    return top + second


def _route(logits):
    rows = [logits[e:e + 1, :] for e in range(N_EXP)]
    mx = functools.reduce(jnp.maximum, rows)
    ex = [jnp.exp(r - mx) for r in rows]
    den = functools.reduce(lambda a, b: a + b, ex)
    p = [e / den for e in ex]
    score = [_top2_of4(*p[EPG * g:EPG * (g + 1)]) for g in range(N_GRP)]
    best, sel = score[0], jnp.zeros_like(score[0], dtype=jnp.int32)
    for g in range(1, N_GRP):
        better = score[g] > best
        best = jnp.where(better, score[g], best)
        sel = jnp.where(better, g, sel)
    cand = []
    for k in range(EPG):
        val = p[k]
        for g in range(1, N_GRP):
            val = jnp.where(sel == g, p[EPG * g + k], val)
        cand.append(val)
    p1, i1 = cand[0], jnp.zeros_like(sel)
    for k in range(1, EPG):
        better = cand[k] > p1
        p1 = jnp.where(better, cand[k], p1)
        i1 = jnp.where(better, k, i1)
    p2, i2 = jnp.full_like(p1, -1.0), jnp.zeros_like(sel)
    for k in range(EPG):
        better = (cand[k] > p2) & (i1 != k)
        p2 = jnp.where(better, cand[k], p2)
        i2 = jnp.where(better, k, i2)
    tot = p1 + p2
    ids = jnp.concatenate([sel * EPG + i1, sel * EPG + i2], axis=0)
    gates = jnp.concatenate([p1 / tot, p2 / tot], axis=0)
    return ids, gates


def _outproj_body(ya_ref, ys_ref, w_ref, h_ref, g_ref, b_ref, rwh_ref, rwl_ref, rb_ref,
                  hf_ref, hp_ref, id_ref, gt_ref):
    mix = jnp.dot(ya_ref[...], w_ref[:ATT_W, :], preferred_element_type=jnp.float32)
    mix = mix + jnp.dot(ys_ref[...], w_ref[ATT_W:, :], preferred_element_type=jnp.float32)
    h1 = _ln(ALPHA * h_ref[...] + mix, g_ref[...], b_ref[...])
    hf_ref[...] = h1
    _pack_rows(h1, hp_ref)
    hh = h1.astype(jnp.bfloat16)
    hl = (h1 - hh.astype(jnp.float32)).astype(jnp.bfloat16)
    nt = (((1,), (1,)), ((), ()))
    logits = lax.dot_general(rwh_ref[...], hh, nt, preferred_element_type=jnp.float32)
    logits = logits + lax.dot_general(rwh_ref[...], hl, nt, preferred_element_type=jnp.float32)
    logits = logits + lax.dot_general(rwl_ref[...], hh, nt, preferred_element_type=jnp.float32)
    ids, gates = _route(logits + rb_ref[...])
    id_ref[...] = ids
    gt_ref[...] = gates


def _outproj(ya, ys, wb, hf, g, b, rwh, rwl, rb):
    n = ya.shape[0]
    t = 512
    return pl.pallas_call(
        _outproj_body,
        grid=(n // t,),
        in_specs=[pl.BlockSpec((t, ATT_W), lambda i: (i, 0)),
                  pl.BlockSpec((t, SGU_W), lambda i: (i, 0)),
                  pl.BlockSpec((D, D), lambda i: (0, 0)),
                  pl.BlockSpec((t, D), lambda i: (i, 0)),
                  pl.BlockSpec((1, D), lambda i: (0, 0)),
                  pl.BlockSpec((1, D), lambda i: (0, 0)),
                  pl.BlockSpec((N_EXP, D), lambda i: (0, 0)),
                  pl.BlockSpec((N_EXP, D), lambda i: (0, 0)),
                  pl.BlockSpec((N_EXP, 1), lambda i: (0, 0))],
        out_specs=[pl.BlockSpec((t, D), lambda i: (i, 0)),
                   pl.BlockSpec((t * WPT, LANES), lambda i: (i, 0)),
                   pl.BlockSpec((2, t), lambda i: (0, i)),
                   pl.BlockSpec((2, t), lambda i: (0, i))],
        out_shape=[jax.ShapeDtypeStruct((n, D), jnp.float32),
                   jax.ShapeDtypeStruct((n * WPT, LANES), jnp.float32),
                   jax.ShapeDtypeStruct((2, n), jnp.int32),
                   jax.ShapeDtypeStruct((2, n), jnp.float32)],
        compiler_params=_cp(("arbitrary",)),
        name="out_proj_ln_route",
    )(ya, ys, wb, hf, g.reshape(1, D), b.reshape(1, D), rwh, rwl, rb)


def _weight_copies(pairs, layer, e, sem):
    copies = []
    for w_ref, stage, _ in pairs:
        rows = stage.shape[0] // W_CHUNKS
        for c in range(W_CHUNKS):
            copies.append(pltpu.make_async_copy(w_ref.at[layer, e, pl.ds(c * rows, rows)],
                                                stage.at[pl.ds(c * rows, rows)], sem))
    return copies


def _refresh_weights(pairs, layer, te_ref, tend_ref, nt, t, sem):
    e = te_ref[t]

    @pl.when((t == 0) | (e != te_ref[jnp.maximum(t - 1, 0)]))
    def _():
        for cp in _weight_copies(pairs, layer, e, sem):
            cp.wait()
        for _, stage, work in pairs:
            def cast(c, carry, stage=stage, work=work):
                rows = pl.ds(pl.multiple_of(c * CAST_ROWS, CAST_ROWS), CAST_ROWS)
                work[rows, :] = stage[rows, :].astype(jnp.bfloat16)
                return carry

            lax.fori_loop(0, stage.shape[0] // CAST_ROWS, cast, 0)
        nxt = tend_ref[e]

        @pl.when(nxt < nt)
        def _():
            e_next = te_ref[jnp.minimum(nxt, te_ref.shape[0] - 1)]
            for cp in _weight_copies(pairs, layer, e_next, sem):
                cp.start(priority=W_PRIORITY)


def _experts_body(te_ref, nt_ref, tend_ref, src_ref, dst_ref,
                  hp_ref, wg_ref, wu_ref, wd_ref, y_ref,
                  xa, xb, ya, yb, wgs, wus, wds, wgb, wub, wdb, sems, *, layer, n):
    step = pl.program_id(0)
    nt = nt_ref[0]
    gsem = (sems.at[0], sems.at[1])
    ssem = (sems.at[2], sems.at[3])
    wsem = sems.at[4]
    pairs = ((wg_ref, wgs, wgb), (wu_ref, wus, wub), (wd_ref, wds, wdb))
    tile_rows = TM * WPT

    def gather_copy(row, r, xbuf, sem):
        tok = pl.multiple_of(src_ref[row] * WPT, WPT)
        return pltpu.make_async_copy(hp_ref.at[pl.ds(tok, WPT)],
                                     xbuf.at[pl.ds(r * PITCH, WPT)], sem)

    def scatter_copy(entry, r, ybuf, sem):
        dst = pl.multiple_of(dst_ref[entry] * WPT, WPT)
        return pltpu.make_async_copy(ybuf.at[pl.ds(r * PITCH, WPT)],
                                     y_ref.at[pl.ds(dst, WPT)], sem)

    def wait_gather(xbuf, sem):
        pltpu.make_async_copy(hp_ref.at[pl.ds(0, tile_rows)], xbuf.at[pl.ds(0, tile_rows)], sem).wait()

    def wait_scatter(ybuf, sem):
        pltpu.make_async_copy(ybuf.at[pl.ds(0, tile_rows)], y_ref.at[pl.ds(0, tile_rows)], sem).wait()

    @pl.when(step == 0)
    def _():
        for cp in _weight_copies(pairs, layer, te_ref[0], wsem):
            cp.start(priority=W_PRIORITY)

        def issue(r, carry):
            gather_copy(r, r, xa, gsem[0]).start()
            return carry

        lax.fori_loop(0, TM, issue, 0)
        ya[...] = jnp.zeros_like(ya)
        yb[...] = jnp.zeros_like(yb)
        pltpu.make_async_copy(ya.at[pl.ds(0, tile_rows)], y_ref.at[pl.ds(2 * n * WPT, tile_rows)], ssem[0]).start()

    def phase(t, xcur, gcur, xnext, gnext, ycur, scur, yprev, sprev):
        @pl.when(t < nt)
        def _():
            _refresh_weights(pairs, layer, te_ref, tend_ref, nt, t, wsem)
            wait_gather(xcur, gcur)
            tnext = jnp.minimum(t + 1, nt - 1)
            for r in range(TM):
                gather_copy(tnext * TM + r, r, xnext, gnext).start()
            for r in range(TM):
                scatter_copy(t * TM + r, r, yprev, sprev).start()
            x = jnp.concatenate([c.astype(jnp.bfloat16) for c in _unpack_rows(xcur, TM, PITCH)],
                                axis=1)
            g = jnp.dot(x, wgb[...], preferred_element_type=jnp.float32)
            u = jnp.dot(x, wub[...], preferred_element_type=jnp.float32)
            hid = (g * jax.nn.sigmoid(g) * u).astype(jnp.bfloat16)
            y = jnp.dot(hid, wdb[...], preferred_element_type=jnp.float32)
            wait_scatter(ycur, scur)
            _pack_rows(y, ycur, PITCH)

            @pl.when(t == nt - 1)
            def _():
                wait_gather(xnext, gnext)
                wait_scatter(yprev, sprev)

                def issue(r, carry):
                    scatter_copy((t + 1) * TM + r, r, ycur, scur).start()
                    return carry

                lax.fori_loop(0, TM, issue, 0)
                wait_scatter(ycur, scur)

    phase(2 * step, xa, gsem[0], xb, gsem[1], ya, ssem[0], yb, ssem[1])
    phase(2 * step + 1, xb, gsem[1], xa, gsem[0], yb, ssem[1], ya, ssem[0])


def _experts(tile_expert, ntiles, tile_end, src_tok, dst_ext, hp, w_gate, w_up, w_down,
             layer, n, max_tiles):
    any_spec = pl.BlockSpec(memory_space=pl.ANY)
    slab = pltpu.VMEM((TM * PITCH, LANES), jnp.float32)
    return pl.pallas_call(
        functools.partial(_experts_body, layer=layer, n=n),
        grid_spec=pltpu.PrefetchScalarGridSpec(
            num_scalar_prefetch=5,
            grid=(max_tiles // 2,),
            in_specs=[any_spec, any_spec, any_spec, any_spec],
            out_specs=any_spec,
            scratch_shapes=[slab, slab, slab, slab,
                            pltpu.VMEM((D, F), jnp.float32),
                            pltpu.VMEM((D, F), jnp.float32),
                            pltpu.VMEM((F, D), jnp.float32),
                            pltpu.VMEM((D, F), jnp.bfloat16),
                            pltpu.VMEM((D, F), jnp.bfloat16),
                            pltpu.VMEM((F, D), jnp.bfloat16),
                            pltpu.SemaphoreType.DMA((5,))]),
        out_shape=jax.ShapeDtypeStruct(((2 * n + 2 * TM) * WPT, LANES), jnp.float32),
        compiler_params=_cp(("arbitrary",), disable_bounds_checks=True),
        name="experts",
    )(tile_expert, ntiles, tile_end, src_tok, dst_ext, hp, w_gate, w_up, w_down)


def _combine_body(y0_ref, y1_ref, h_ref, gt_ref, g_ref, b_ref, hf_ref, hb_ref):
    t = h_ref.shape[0]
    y0 = jnp.concatenate(_unpack_rows(y0_ref, t), axis=1)
    y1 = jnp.concatenate(_unpack_rows(y1_ref, t), axis=1)
    gt = gt_ref[...]
    ffn = gt[:, 0:1] * y0 + gt[:, 1:2] * y1
    h2 = _ln(ALPHA * h_ref[...] + ffn, g_ref[...], b_ref[...])
    hf_ref[...] = h2
    hb_ref[...] = h2.astype(jnp.bfloat16)


def _combine(yg, hf, gates_t, g, b):
    n = hf.shape[0]
    t = 256
    nb = n // t
    return pl.pallas_call(
        _combine_body,
        grid=(nb,),
        in_specs=[pl.BlockSpec((t * WPT, LANES), lambda i: (i, 0)),
                  pl.BlockSpec((t * WPT, LANES), lambda i: (nb + i, 0)),
                  pl.BlockSpec((t, D), lambda i: (i, 0)),
                  pl.BlockSpec((t, 2), lambda i: (i, 0)),
                  pl.BlockSpec((1, D), lambda i: (0, 0)),
                  pl.BlockSpec((1, D), lambda i: (0, 0))],
        out_specs=[pl.BlockSpec((t, D), lambda i: (i, 0)),
                   pl.BlockSpec((t, D), lambda i: (i, 0))],
        out_shape=[jax.ShapeDtypeStruct((n, D), jnp.float32),
                   jax.ShapeDtypeStruct((n, D), jnp.bfloat16)],
        compiler_params=_cp(("arbitrary",)),
        name="combine_ln",
    )(yg, yg, hf, gates_t, g.reshape(1, D), b.reshape(1, D))


def _dispatch_plan(ids, n, max_tiles):
    i32 = jnp.int32
    rows = max_tiles * TM
    e_flat = ids.reshape(-1)
    onehot = (e_flat[:, None] == jnp.arange(N_EXP, dtype=i32)[None, :]).astype(i32)
    csum = jnp.cumsum(onehot, axis=0)
    counts = csum[-1]
    tiles = (counts + TM - 1) // TM
    tile_end = jnp.cumsum(tiles).astype(i32)
    tile_start = tile_end - tiles
    pos = jnp.sum(onehot * (csum - 1 + (tile_start * TM)[None, :]), axis=1)
    asg = jnp.full((rows,), -1, i32).at[pos].set(jnp.arange(2 * n, dtype=i32))
    r = jnp.arange(rows, dtype=i32)
    valid = asg >= 0
    src_tok = jnp.where(valid, jnp.where(asg >= n, asg - n, asg), 0)
    dump = 2 * n + ((r // TM) % 2) * TM + (r % TM)
    dst_row = jnp.where(valid, asg, dump)
    dst_ext = jnp.concatenate([2 * n + TM + jnp.arange(TM, dtype=i32), dst_row])
    tile_ids = jnp.arange(max_tiles, dtype=i32)
    tile_expert = jnp.minimum(
        jnp.sum((tile_ids[:, None] >= tile_end[None, :]).astype(i32), axis=1), N_EXP - 1)
    return src_tok, dst_ext, tile_expert, tile_end[-1:], tile_end


def kernel(x, ln_in_g, ln_in_b, w_in, rel_bias, sgu_ln_g, sgu_ln_b, sgu_w, sgu_b, w_out,
           ln1_g, ln1_b, router_w, router_b, w_gate, w_up, w_down, ln2_g, ln2_b):
    batch, seq, _ = x.shape
    n = batch * seq
    max_tiles = (2 * n) // TM + N_EXP
    rwt = router_w.T
    rwh = rwt.astype(jnp.bfloat16)
    rwl = (rwt - rwh.astype(jnp.float32)).astype(jnp.bfloat16)
    rb = router_b.reshape(N_EXP, 1)

    hf, hb = _ln0(x.reshape(n, D), ln_in_g, ln_in_b)
    for l in range(DEPTH):
        z = _inproj(hb, w_in, l)
        ya = _attention(z, _attn_bias(rel_bias[l]), batch, seq)
        ys = _sgu(z, sgu_ln_g[l], sgu_ln_b[l], sgu_w[l], sgu_b[l])
        hf, hp, ids, gates = _outproj(ya, ys, w_out[l].astype(jnp.bfloat16), hf,
                                      ln1_g[l], ln1_b[l], rwh, rwl, rb)
        src_tok, dst_ext, tile_expert, ntiles, tile_end = _dispatch_plan(ids, n, max_tiles)
        yslots = _experts(tile_expert, ntiles, tile_end, src_tok, dst_ext, hp,
                          w_gate, w_up, w_down, l, n, max_tiles)
        hf, hb = _combine(yslots, hf, gates.T, ln2_g[l], ln2_b[l])
    return hf.reshape(batch, seq, D)
```

```python
import functools
import math

import jax
import jax.numpy as jnp
import numpy as np
from jax import lax
from jax.experimental import pallas as pl
from jax.experimental.pallas import tpu as pltpu

D = 2048
HEADS = 8
DH = 128
ATT_W = HEADS * DH
SGU_G = 8
SGU_W = SGU_G * 128
WIN = 128
IN_W = 3 * ATT_W + 2 * SGU_W
CHUNK = 64
N_PREV = 8
REL_CLIP = 128
N_EXP = 16
N_GRP = 4
EPG = 4
F = 1024
DEPTH = 2
ALPHA = (2.0 * DEPTH) ** 0.25
EPS = 1e-5
NEG = -1e30

LANES = 128
SUBLANES = 8
VMEM_LIMIT = 56 * 1024 * 1024

QB = 256
KWIN = QB + N_PREV * CHUNK
TM = 256
WPT = D // LANES
W_CHUNKS = 4
CAST_ROWS = 128
PITCH = WPT + 1
W_PRIORITY = 1


def _cp(sem, vmem=VMEM_LIMIT, **kw):
    return pltpu.CompilerParams(dimension_semantics=sem, vmem_limit_bytes=vmem, **kw)


def _ln(x, g, b):
    mu = jnp.mean(x, axis=-1, keepdims=True)
    xc = x - mu
    var = jnp.mean(xc * xc, axis=-1, keepdims=True)
    return xc * lax.rsqrt(var + EPS) * g + b


def _pack_rows(y, o_ref):
    t = y.shape[0]
    for s in range(WPT):
        o_ref[pl.ds(s, t, stride=PITCH), :] = y[:, LANES * s:LANES * (s + 1)]
    for s in range(WPT, PITCH):
        o_ref[pl.ds(s, t, stride=PITCH), :] = jnp.zeros((t, LANES), jnp.float32)


def _unpack_rows(p_ref, t):
    return [p_ref[pl.ds(s, t, stride=PITCH), :] for s in range(WPT)]


def _ln0_body(x_ref, g_ref, b_ref, hb_ref):
    hb_ref[...] = _ln(x_ref[...], g_ref[...], b_ref[...]).astype(jnp.bfloat16)


def _ln0(x, g, b):
    n = x.shape[0]
    t = 256
    return pl.pallas_call(
        _ln0_body,
        grid=(n // t,),
        in_specs=[pl.BlockSpec((t, D), lambda i: (i, 0)),
                  pl.BlockSpec((1, D), lambda i: (0, 0)),
                  pl.BlockSpec((1, D), lambda i: (0, 0))],
        out_specs=pl.BlockSpec((t, D), lambda i: (i, 0)),
        out_shape=jax.ShapeDtypeStruct((n, D), jnp.bfloat16),
        compiler_params=_cp(("arbitrary",)),
        name="ln_in",
    )(x, g.reshape(1, D), b.reshape(1, D))


def _inproj_body(x_ref, w_ref, o_ref, wb_ref):
    @pl.when(pl.program_id(1) == 0)
    def _():
        wb_ref[...] = w_ref[0].astype(jnp.bfloat16)

    o_ref[...] = jnp.dot(x_ref[...], wb_ref[...],
                         preferred_element_type=jnp.float32).astype(o_ref.dtype)


def _inproj(hb, w, layer):
    n = hb.shape[0]
    tm, tn = 1024, 1024
    return pl.pallas_call(
        _inproj_body,
        grid=(IN_W // tn, n // tm),
        in_specs=[pl.BlockSpec((tm, D), lambda j, i: (i, 0)),
                  pl.BlockSpec((1, D, tn), lambda j, i: (layer, 0, j))],
        out_specs=pl.BlockSpec((tm, tn), lambda j, i: (i, j)),
        out_shape=jax.ShapeDtypeStruct((n, IN_W), jnp.bfloat16),
        scratch_shapes=[pltpu.VMEM((D, tn), jnp.bfloat16)],
        compiler_params=_cp(("arbitrary", "arbitrary")),
        name="in_proj",
    )(hb, w)


def _attn_body(q_ref, k_ref, v_ref, bm_ref, o_ref):
    seq = q_ref.shape[0]
    scale = 1.0 / math.sqrt(DH)
    for j in range(seq // QB):
        q0 = j * QB
        k0 = max(0, q0 - N_PREV * CHUNK)
        nk = q0 + QB - k0
        q = q_ref[q0:q0 + QB, :]
        k = k_ref[k0:k0 + nk, :]
        v = v_ref[k0:k0 + nk, :]
        s = lax.dot_general(q, k, (((1,), (1,)), ((), ())),
                            preferred_element_type=jnp.float32)
        s = s * scale + bm_ref[0, :, KWIN - nk:]
        m = jnp.max(s, axis=-1, keepdims=True)
        p = jnp.exp(s - m)
        l = jnp.sum(p, axis=-1, keepdims=True)
        o = jnp.dot(p.astype(jnp.bfloat16), v, preferred_element_type=jnp.float32)
        o_ref[q0:q0 + QB, :] = (o / l).astype(o_ref.dtype)


def _attn_bias(rel_table):
    period = KWIN + QB
    off = N_PREV * CHUNK
    tab = rel_table.astype(jnp.float32)
    heads = tab.shape[0]
    far = tab[:, 2 * REL_CLIP:]
    n_far = off - REL_CLIP + 1
    near = tab[:, 1:2 * REL_CLIP][:, ::-1]
    n_low = KWIN - n_far - near.shape[1]
    g = jnp.concatenate([jnp.broadcast_to(far, (heads, n_far)), near,
                         jnp.broadcast_to(tab[:, :1], (heads, n_low)),
                         jnp.broadcast_to(far, (heads, period - KWIN))], axis=1)
    bias = jnp.tile(g, (1, QB + 1))[:, :QB * (period - 1)].reshape(heads, QB, period - 1)
    bias = bias[:, :, :KWIN]
    i = np.arange(QB)[:, None]
    c = np.arange(KWIN)[None, :]
    qc = i // CHUNK
    kc = c // CHUNK
    band = (kc >= qc) & (kc <= qc + N_PREV)
    return jnp.where(jnp.asarray(band)[None], bias, NEG)


def _attention(z, bm, batch, seq):
    return pl.pallas_call(
        _attn_body,
        grid=(batch, HEADS),
        in_specs=[pl.BlockSpec((seq, DH), lambda b, h: (b, h)),
                  pl.BlockSpec((seq, DH), lambda b, h: (b, HEADS + h)),
                  pl.BlockSpec((seq, DH), lambda b, h: (b, 2 * HEADS + h)),
                  pl.BlockSpec((1, QB, KWIN), lambda b, h: (h, 0, 0))],
        out_specs=pl.BlockSpec((seq, DH), lambda b, h: (b, h)),
        out_shape=jax.ShapeDtypeStruct((batch * seq, ATT_W), jnp.bfloat16),
        compiler_params=_cp(("arbitrary", "arbitrary")),
        name="band_attn",
    )(z, z, z, bm)


def _gelu(x):
    return 0.5 * x * (1.0 + lax.erf(x * (1.0 / math.sqrt(2.0))))


def _sgu_body(u_ref, v_ref, g_ref, b_ref, w_ref, bs_ref, o_ref):
    t = u_ref.shape[0]
    u = _gelu(u_ref[...].astype(jnp.float32))
    v = _gelu(v_ref[...].astype(jnp.float32))
    vn = _ln(v, g_ref[...], b_ref[...]).astype(jnp.bfloat16)
    row = lax.broadcasted_iota(jnp.int32, (WIN, WIN), 0)
    col = lax.broadcasted_iota(jnp.int32, (WIN, WIN), 1)
    causal = row >= col
    for g in range(SGU_G):
        wm = jnp.where(causal, w_ref[g], 0.0).astype(jnp.bfloat16)
        bias = bs_ref[:, g:g + 1]
        for n in range(t // WIN):
            r0, c0 = n * WIN, g * 128
            mixed = jnp.dot(wm, vn[r0:r0 + WIN, c0:c0 + 128],
                            preferred_element_type=jnp.float32) + bias
            o_ref[r0:r0 + WIN, c0:c0 + 128] = (u[r0:r0 + WIN, c0:c0 + 128] * mixed
                                               ).astype(o_ref.dtype)


def _sgu(z, ln_g, ln_b, w_s, b_s):
    n = z.shape[0]
    t = 256
    ublk = 3 * ATT_W // SGU_W
    return pl.pallas_call(
        _sgu_body,
        grid=(n // t,),
        in_specs=[pl.BlockSpec((t, SGU_W), lambda i: (i, ublk)),
                  pl.BlockSpec((t, SGU_W), lambda i: (i, ublk + 1)),
                  pl.BlockSpec((1, SGU_W), lambda i: (0, 0)),
                  pl.BlockSpec((1, SGU_W), lambda i: (0, 0)),
                  pl.BlockSpec((SGU_G, WIN, WIN), lambda i: (0, 0, 0)),
                  pl.BlockSpec((WIN, SGU_G), lambda i: (0, 0))],
        out_specs=pl.BlockSpec((t, SGU_W), lambda i: (i, 0)),
        out_shape=jax.ShapeDtypeStruct((n, SGU_W), jnp.bfloat16),
        compiler_params=_cp(("arbitrary",)),
        name="sgu",
    )(z, z, ln_g.reshape(1, SGU_W), ln_b.reshape(1, SGU_W), w_s, b_s.T)


def _top2_of4(a, b, c, d):
    m1, n1 = jnp.maximum(a, b), jnp.minimum(a, b)
    m2, n2 = jnp.maximum(c, d), jnp.minimum(c, d)
    top = jnp.maximum(m1, m2)
    second = jnp.maximum(jnp.minimum(m1, m2), jnp.maximum(n1, n2))
    return top + second


def _route(logits):
    rows = [logits[e:e + 1, :] for e in range(N_EXP)]
    mx = functools.reduce(jnp.maximum, rows)
    ex = [jnp.exp(r - mx) for r in rows]
    den = functools.reduce(lambda a, b: a + b, ex)
    p = [e / den for e in ex]
    score = [_top2_of4(*p[EPG * g:EPG * (g + 1)]) for g in range(N_GRP)]
    best, sel = score[0], jnp.zeros_like(score[0], dtype=jnp.int32)
    for g in range(1, N_GRP):
        better = score[g] > best
        best = jnp.where(better, score[g], best)
        sel = jnp.where(better, g, sel)
    cand = []
    for k in range(EPG):
        val = p[k]
        for g in range(1, N_GRP):
            val = jnp.where(sel == g, p[EPG * g + k], val)
        cand.append(val)
    p1, i1 = cand[0], jnp.zeros_like(sel)
    for k in range(1, EPG):
        better = cand[k] > p1
        p1 = jnp.where(better, cand[k], p1)
        i1 = jnp.where(better, k, i1)
    p2, i2 = jnp.full_like(p1, -1.0), jnp.zeros_like(sel)
    for k in range(EPG):
        better = (cand[k] > p2) & (i1 != k)
        p2 = jnp.where(better, cand[k], p2)
        i2 = jnp.where(better, k, i2)
    tot = p1 + p2
    ids = jnp.concatenate([sel * EPG + i1, sel * EPG + i2], axis=0)
    gates = jnp.concatenate([p1 / tot, p2 / tot], axis=0)
    return ids, gates


def _outproj_body(ya_ref, ys_ref, w_ref, h_ref, gin_ref, bin_ref, g_ref, b_ref,
                  rwh_ref, rwl_ref, rb_ref, hf_ref, hp_ref, id_ref, gt_ref, *, raw_input):
    mix = jnp.dot(ya_ref[...], w_ref[:ATT_W, :], preferred_element_type=jnp.float32)
    mix = mix + jnp.dot(ys_ref[...], w_ref[ATT_W:, :], preferred_element_type=jnp.float32)
    h = h_ref[...]
    if raw_input:
        h = _ln(h, gin_ref[...], bin_ref[...])
    h1 = _ln(ALPHA * h + mix, g_ref[...], b_ref[...])
    hf_ref[...] = h1
    _pack_rows(h1, hp_ref)
    hh = h1.astype(jnp.bfloat16)
    hl = (h1 - hh.astype(jnp.float32)).astype(jnp.bfloat16)
    nt = (((1,), (1,)), ((), ()))
    logits = lax.dot_general(rwh_ref[...], hh, nt, preferred_element_type=jnp.float32)
    logits = logits + lax.dot_general(rwh_ref[...], hl, nt, preferred_element_type=jnp.float32)
    logits = logits + lax.dot_general(rwl_ref[...], hh, nt, preferred_element_type=jnp.float32)
    ids, gates = _route(logits + rb_ref[...])
    id_ref[...] = ids
    gt_ref[...] = gates


def _outproj(ya, ys, wb, resid, gin, bin_, g, b, rwh, rwl, rb, raw_input):
    n = ya.shape[0]
    t = 512
    return pl.pallas_call(
        functools.partial(_outproj_body, raw_input=raw_input),
        grid=(n // t,),
        in_specs=[pl.BlockSpec((t, ATT_W), lambda i: (i, 0)),
                  pl.BlockSpec((t, SGU_W), lambda i: (i, 0)),
                  pl.BlockSpec((D, D), lambda i: (0, 0)),
                  pl.BlockSpec((t, D), lambda i: (i, 0)),
                  pl.BlockSpec((1, D), lambda i: (0, 0)),
                  pl.BlockSpec((1, D), lambda i: (0, 0)),
                  pl.BlockSpec((1, D), lambda i: (0, 0)),
                  pl.BlockSpec((1, D), lambda i: (0, 0)),
                  pl.BlockSpec((N_EXP, D), lambda i: (0, 0)),
                  pl.BlockSpec((N_EXP, D), lambda i: (0, 0)),
                  pl.BlockSpec((N_EXP, 1), lambda i: (0, 0))],
        out_specs=[pl.BlockSpec((t, D), lambda i: (i, 0)),
                   pl.BlockSpec((t * PITCH, LANES), lambda i: (i, 0)),
                   pl.BlockSpec((2, t), lambda i: (0, i)),
                   pl.BlockSpec((2, t), lambda i: (0, i))],
        out_shape=[jax.ShapeDtypeStruct((n, D), jnp.float32),
                   jax.ShapeDtypeStruct((n * PITCH, LANES), jnp.float32),
                   jax.ShapeDtypeStruct((2, n), jnp.int32),
                   jax.ShapeDtypeStruct((2, n), jnp.float32)],
        compiler_params=_cp(("arbitrary",)),
        name="out_proj_ln_route",
    )(ya, ys, wb, resid, gin.reshape(1, D), bin_.reshape(1, D), g.reshape(1, D),
      b.reshape(1, D), rwh, rwl, rb)


def _weight_copies(pairs, layer, e, sem):
    copies = []
    for w_ref, stage, _ in pairs:
        rows = stage.shape[0] // W_CHUNKS
        for c in range(W_CHUNKS):
            copies.append(pltpu.make_async_copy(w_ref.at[layer, e, pl.ds(c * rows, rows)],
                                                stage.at[pl.ds(c * rows, rows)], sem))
    return copies


def _refresh_weights(pairs, layer, te_ref, tend_ref, nt, t, sem):
    e = te_ref[t]

    @pl.when((t == 0) | (e != te_ref[jnp.maximum(t - 1, 0)]))
    def _():
        for cp in _weight_copies(pairs, layer, e, sem):
            cp.wait()
        for _, stage, work in pairs:
            def cast(c, carry, stage=stage, work=work):
                rows = pl.ds(pl.multiple_of(c * CAST_ROWS, CAST_ROWS), CAST_ROWS)
                work[rows, :] = stage[rows, :].astype(jnp.bfloat16)
                return carry

            lax.fori_loop(0, stage.shape[0] // CAST_ROWS, cast, 0)
        nxt = tend_ref[e]

        @pl.when(nxt < nt)
        def _():
            e_next = te_ref[jnp.minimum(nxt, te_ref.shape[0] - 1)]
            for cp in _weight_copies(pairs, layer, e_next, sem):
                cp.start(priority=W_PRIORITY)


def _experts_body(te_ref, nt_ref, tend_ref, src_ref, dst_ref,
                  hp_ref, wg_ref, wu_ref, wd_ref, y_ref,
                  xa, xb, ya, yb, wgs, wus, wds, wgb, wub, wdb, sems, *, layer, n):
    step = pl.program_id(0)
    nt = nt_ref[0]
    gsem = (sems.at[0], sems.at[1])
    ssem = (sems.at[2], sems.at[3])
    wsem = sems.at[4]
    pairs = ((wg_ref, wgs, wgb), (wu_ref, wus, wub), (wd_ref, wds, wdb))
    tile_rows = TM * PITCH

    def gather_copy(row, r, xbuf, sem):
        return pltpu.make_async_copy(hp_ref.at[pl.ds(src_ref[row] * PITCH, PITCH)],
                                     xbuf.at[pl.ds(r * PITCH, PITCH)], sem)

    def scatter_copy(entry, r, ybuf, sem):
        return pltpu.make_async_copy(ybuf.at[pl.ds(r * PITCH, PITCH)],
                                     y_ref.at[pl.ds(dst_ref[entry] * PITCH, PITCH)], sem)

    def wait_gather(xbuf, sem):
        pltpu.make_async_copy(hp_ref.at[pl.ds(0, tile_rows)], xbuf, sem).wait()

    def wait_scatter(ybuf, sem):
        pltpu.make_async_copy(ybuf, y_ref.at[pl.ds(0, tile_rows)], sem).wait()

    @pl.when(step == 0)
    def _():
        for cp in _weight_copies(pairs, layer, te_ref[0], wsem):
            cp.start(priority=W_PRIORITY)

        def issue(r, carry):
            gather_copy(r, r, xa, gsem[0]).start()
            return carry

        lax.fori_loop(0, TM, issue, 0)
        ya[...] = jnp.zeros_like(ya)
        yb[...] = jnp.zeros_like(yb)
        pltpu.make_async_copy(ya, y_ref.at[pl.ds(2 * n * PITCH, tile_rows)], ssem[0]).start()

    def phase(t, xcur, gcur, xnext, gnext, ycur, scur, yprev, sprev):
        @pl.when(t < nt)
        def _():
            _refresh_weights(pairs, layer, te_ref, tend_ref, nt, t, wsem)
            wait_gather(xcur, gcur)
            tnext = jnp.minimum(t + 1, nt - 1)
            for r in range(TM):
                gather_copy(tnext * TM + r, r, xnext, gnext).start()
            for r in range(TM):
                scatter_copy(t * TM + r, r, yprev, sprev).start()
            x = jnp.concatenate([c.astype(jnp.bfloat16) for c in _unpack_rows(xcur, TM)],
                                axis=1)
            g = jnp.dot(x, wgb[...], preferred_element_type=jnp.float32)
            u = jnp.dot(x, wub[...], preferred_element_type=jnp.float32)
            hid = (g * jax.nn.sigmoid(g) * u).astype(jnp.bfloat16)
            y = jnp.dot(hid, wdb[...], preferred_element_type=jnp.float32)
            wait_scatter(ycur, scur)
            _pack_rows(y, ycur)

            @pl.when(t == nt - 1)
            def _():
                wait_gather(xnext, gnext)
                wait_scatter(yprev, sprev)

                def issue(r, carry):
                    scatter_copy((t + 1) * TM + r, r, ycur, scur).start()
                    return carry

                lax.fori_loop(0, TM, issue, 0)
                wait_scatter(ycur, scur)

    phase(2 * step, xa, gsem[0], xb, gsem[1], ya, ssem[0], yb, ssem[1])
    phase(2 * step + 1, xb, gsem[1], xa, gsem[0], yb, ssem[1], ya, ssem[0])


def _experts(tile_expert, ntiles, tile_end, src_tok, dst_ext, hp, w_gate, w_up, w_down,
             layer, n, max_tiles):
    any_spec = pl.BlockSpec(memory_space=pl.ANY)
    slab = pltpu.VMEM((TM * PITCH, LANES), jnp.float32)
    return pl.pallas_call(
        functools.partial(_experts_body, layer=layer, n=n),
        grid_spec=pltpu.PrefetchScalarGridSpec(
            num_scalar_prefetch=5,
            grid=(max_tiles // 2,),
            in_specs=[any_spec, any_spec, any_spec, any_spec],
            out_specs=any_spec,
            scratch_shapes=[slab, slab, slab, slab,
                            pltpu.VMEM((D, F), jnp.float32),
                            pltpu.VMEM((D, F), jnp.float32),
                            pltpu.VMEM((F, D), jnp.float32),
                            pltpu.VMEM((D, F), jnp.bfloat16),
                            pltpu.VMEM((D, F), jnp.bfloat16),
                            pltpu.VMEM((F, D), jnp.bfloat16),
                            pltpu.SemaphoreType.DMA((5,))]),
        out_shape=jax.ShapeDtypeStruct(((2 * n + 2 * TM) * PITCH, LANES), jnp.float32),
        compiler_params=_cp(("arbitrary",), disable_bounds_checks=True),
        name="experts",
    )(tile_expert, ntiles, tile_end, src_tok, dst_ext, hp, w_gate, w_up, w_down)


def _combine_body(y0_ref, y1_ref, h_ref, gt_ref, g_ref, b_ref, hf_ref, hb_ref):
    t = h_ref.shape[0]
    y0 = jnp.concatenate(_unpack_rows(y0_ref, t), axis=1)
    y1 = jnp.concatenate(_unpack_rows(y1_ref, t), axis=1)
    gt = gt_ref[...]
    ffn = gt[:, 0:1] * y0 + gt[:, 1:2] * y1
    h2 = _ln(ALPHA * h_ref[...] + ffn, g_ref[...], b_ref[...])
    hf_ref[...] = h2
    hb_ref[...] = h2.astype(jnp.bfloat16)


def _combine(yg, hf, gates_t, g, b):
    n = hf.shape[0]
    t = 256
    nb = n // t
    return pl.pallas_call(
        _combine_body,
        grid=(nb,),
        in_specs=[pl.BlockSpec((t * PITCH, LANES), lambda i: (i, 0)),
                  pl.BlockSpec((t * PITCH, LANES), lambda i: (nb + i, 0)),
                  pl.BlockSpec((t, D), lambda i: (i, 0)),
                  pl.BlockSpec((t, 2), lambda i: (i, 0)),
                  pl.BlockSpec((1, D), lambda i: (0, 0)),
                  pl.BlockSpec((1, D), lambda i: (0, 0))],
        out_specs=[pl.BlockSpec((t, D), lambda i: (i, 0)),
                   pl.BlockSpec((t, D), lambda i: (i, 0))],
        out_shape=[jax.ShapeDtypeStruct((n, D), jnp.float32),
                   jax.ShapeDtypeStruct((n, D), jnp.bfloat16)],
        compiler_params=_cp(("arbitrary",)),
        name="combine_ln",
    )(yg, yg, hf, gates_t, g.reshape(1, D), b.reshape(1, D))


def _dispatch_plan(ids, n, max_tiles):
    i32 = jnp.int32
    rows = max_tiles * TM
    e_flat = ids.reshape(-1)
    onehot = (e_flat[:, None] == jnp.arange(N_EXP, dtype=i32)[None, :]).astype(i32)
    csum = jnp.cumsum(onehot, axis=0)
    counts = csum[-1]
    tiles = (counts + TM - 1) // TM
    tile_end = jnp.cumsum(tiles).astype(i32)
    tile_start = tile_end - tiles
    pos = jnp.sum(onehot * (csum - 1 + (tile_start * TM)[None, :]), axis=1)
    asg = jnp.full((rows,), -1, i32).at[pos].set(jnp.arange(2 * n, dtype=i32))
    r = jnp.arange(rows, dtype=i32)
    valid = asg >= 0
    src_tok = jnp.where(valid, jnp.where(asg >= n, asg - n, asg), 0)
    dump = 2 * n + ((r // TM) % 2) * TM + (r % TM)
    dst_row = jnp.where(valid, asg, dump)
    dst_ext = jnp.concatenate([2 * n + TM + jnp.arange(TM, dtype=i32), dst_row])
    tile_ids = jnp.arange(max_tiles, dtype=i32)
    tile_expert = jnp.minimum(
        jnp.sum((tile_ids[:, None] >= tile_end[None, :]).astype(i32), axis=1), N_EXP - 1)
    return src_tok, dst_ext, tile_expert, tile_end[-1:], tile_end


def kernel(x, ln_in_g, ln_in_b, w_in, rel_bias, sgu_ln_g, sgu_ln_b, sgu_w, sgu_b, w_out,
           ln1_g, ln1_b, router_w, router_b, w_gate, w_up, w_down, ln2_g, ln2_b):
    batch, seq, _ = x.shape
    n = batch * seq
    max_tiles = (2 * n) // TM + N_EXP
    rwt = router_w.T
    rwh = rwt.astype(jnp.bfloat16)
    rwl = (rwt - rwh.astype(jnp.float32)).astype(jnp.bfloat16)
    rb = router_b.reshape(N_EXP, 1)

    hf = x.reshape(n, D)
    hb = _ln0(hf, ln_in_g, ln_in_b)
    for l in range(DEPTH):
        z = _inproj(hb, w_in, l)
        ya = _attention(z, _attn_bias(rel_bias[l]), batch, seq)
        ys = _sgu(z, sgu_ln_g[l], sgu_ln_b[l], sgu_w[l], sgu_b[l])
        hf, hp, ids, gates = _outproj(ya, ys, w_out[l].astype(jnp.bfloat16), hf,
                                      ln_in_g, ln_in_b, ln1_g[l], ln1_b[l], rwh, rwl, rb,
                                      raw_input=(l == 0))
        src_tok, dst_ext, tile_expert, ntiles, tile_end = _dispatch_plan(ids, n, max_tiles)
        yslots = _experts(tile_expert, ntiles, tile_end, src_tok, dst_ext, hp,
                          w_gate, w_up, w_down, l, n, max_tiles)
        hf, hb = _combine(yslots, hf, gates.T, ln2_g[l], ln2_b[l])
    return hf.reshape(batch, seq, D)
```

```python
import functools
import math

import jax
import jax.numpy as jnp
import numpy as np
from jax import lax
from jax.experimental import pallas as pl
from jax.experimental.pallas import tpu as pltpu

D = 2048
HEADS = 8
DH = 128
ATT_W = HEADS * DH
SGU_G = 8
SGU_W = SGU_G * 128
WIN = 128
IN_W = 3 * ATT_W + 2 * SGU_W
CHUNK = 64
N_PREV = 8
REL_CLIP = 128
N_EXP = 16
N_GRP = 4
EPG = 4
F = 1024
DEPTH = 2
ALPHA = (2.0 * DEPTH) ** 0.25
EPS = 1e-5
NEG = -1e30
LOG2E = math.log2(math.e)

LANES = 128
SUBLANES = 8
VMEM_LIMIT = 56 * 1024 * 1024

QB = 256
KWIN = QB + N_PREV * CHUNK
TM = 256
WPT = D // LANES
W_CHUNKS = 4
CAST_ROWS = 128
PITCH = WPT + 1
W_PRIORITY = 1


def _cp(sem, vmem=VMEM_LIMIT, **kw):
    return pltpu.CompilerParams(dimension_semantics=sem, vmem_limit_bytes=vmem, **kw)


def _ln(x, g, b):
    mu = jnp.mean(x, axis=-1, keepdims=True)
    xc = x - mu
    var = jnp.mean(xc * xc, axis=-1, keepdims=True)
    return xc * lax.rsqrt(var + EPS) * g + b


def _pack_rows(y, o_ref, row0=0):
    t = y.shape[0]
    base = row0 * PITCH
    for s in range(WPT):
        o_ref[pl.ds(base + s, t, stride=PITCH), :] = y[:, LANES * s:LANES * (s + 1)]
    for s in range(WPT, PITCH):
        o_ref[pl.ds(base + s, t, stride=PITCH), :] = jnp.zeros((t, LANES), jnp.float32)


def _unpack_rows(p_ref, t):
    return [p_ref[pl.ds(s, t, stride=PITCH), :] for s in range(WPT)]


def _ln0_body(x_ref, g_ref, b_ref, hb_ref):
    hb_ref[...] = _ln(x_ref[...], g_ref[...], b_ref[...]).astype(jnp.bfloat16)


def _ln0(x, g, b):
    n = x.shape[0]
    t = 256
    return pl.pallas_call(
        _ln0_body,
        grid=(n // t,),
        in_specs=[pl.BlockSpec((t, D), lambda i: (i, 0)),
                  pl.BlockSpec((1, D), lambda i: (0, 0)),
                  pl.BlockSpec((1, D), lambda i: (0, 0))],
        out_specs=pl.BlockSpec((t, D), lambda i: (i, 0)),
        out_shape=jax.ShapeDtypeStruct((n, D), jnp.bfloat16),
        compiler_params=_cp(("arbitrary",)),
        name="ln_in",
    )(x, g.reshape(1, D), b.reshape(1, D))


def _inproj_body(x_ref, w_ref, o_ref, wb_ref):
    @pl.when(pl.program_id(1) == 0)
    def _():
        wb_ref[...] = w_ref[0].astype(jnp.bfloat16)

    o_ref[...] = jnp.dot(x_ref[...], wb_ref[...],
                         preferred_element_type=jnp.float32).astype(o_ref.dtype)


def _inproj(hb, w, layer):
    n = hb.shape[0]
    tm, tn = 1024, 1024
    return pl.pallas_call(
        _inproj_body,
        grid=(IN_W // tn, n // tm),
        in_specs=[pl.BlockSpec((tm, D), lambda j, i: (i, 0)),
                  pl.BlockSpec((1, D, tn), lambda j, i: (layer, 0, j))],
        out_specs=pl.BlockSpec((tm, tn), lambda j, i: (i, j)),
        out_shape=jax.ShapeDtypeStruct((n, IN_W), jnp.bfloat16),
        scratch_shapes=[pltpu.VMEM((D, tn), jnp.bfloat16)],
        compiler_params=_cp(("arbitrary", "arbitrary")),
        name="in_proj",
    )(hb, w)


def _attn_body(q_ref, k_ref, v_ref, bm_ref, o_ref):
    seq = q_ref.shape[0]
    scale = LOG2E / math.sqrt(DH)
    for j in range(seq // QB):
        q0 = j * QB
        k0 = max(0, q0 - N_PREV * CHUNK)
        nk = q0 + QB - k0
        q = q_ref[q0:q0 + QB, :]
        k = k_ref[k0:k0 + nk, :]
        v = v_ref[k0:k0 + nk, :]
        s = lax.dot_general(q, k, (((1,), (1,)), ((), ())),
                            preferred_element_type=jnp.float32)
        s = s * scale + bm_ref[0, :, KWIN - nk:]
        m = jnp.max(s, axis=-1, keepdims=True)
        p = jnp.exp2(s - m)
        l = jnp.sum(p, axis=-1, keepdims=True)
        o = jnp.dot(p.astype(jnp.bfloat16), v, preferred_element_type=jnp.float32)
        o_ref[q0:q0 + QB, :] = (o / l).astype(o_ref.dtype)


def _attn_bias(rel_table):
    period = KWIN + QB
    off = N_PREV * CHUNK
    tab = rel_table.astype(jnp.float32)
    heads = tab.shape[0]
    far = tab[:, 2 * REL_CLIP:]
    n_far = off - REL_CLIP + 1
    near = tab[:, 1:2 * REL_CLIP][:, ::-1]
    n_low = KWIN - n_far - near.shape[1]
    g = jnp.concatenate([jnp.broadcast_to(far, (heads, n_far)), near,
                         jnp.broadcast_to(tab[:, :1], (heads, n_low)),
                         jnp.broadcast_to(far, (heads, period - KWIN))], axis=1)
    bias = jnp.tile(g, (1, QB + 1))[:, :QB * (period - 1)].reshape(heads, QB, period - 1)
    bias = bias[:, :, :KWIN]
    i = np.arange(QB)[:, None]
    c = np.arange(KWIN)[None, :]
    qc = i // CHUNK
    kc = c // CHUNK
    band = (kc >= qc) & (kc <= qc + N_PREV)
    return jnp.where(jnp.asarray(band)[None], bias * LOG2E, NEG)


def _attention(z, bm, batch, seq):
    return pl.pallas_call(
        _attn_body,
        grid=(batch, HEADS),
        in_specs=[pl.BlockSpec((seq, DH), lambda b, h: (b, h)),
                  pl.BlockSpec((seq, DH), lambda b, h: (b, HEADS + h)),
                  pl.BlockSpec((seq, DH), lambda b, h: (b, 2 * HEADS + h)),
                  pl.BlockSpec((1, QB, KWIN), lambda b, h: (h, 0, 0))],
        out_specs=pl.BlockSpec((seq, DH), lambda b, h: (b, h)),
        out_shape=jax.ShapeDtypeStruct((batch * seq, ATT_W), jnp.bfloat16),
        compiler_params=_cp(("arbitrary", "arbitrary")),
        name="band_attn",
    )(z, z, z, bm)


def _gelu(x):
    return 0.5 * x * (1.0 + lax.erf(x * (1.0 / math.sqrt(2.0))))


def _sgu_body(u_ref, v_ref, g_ref, b_ref, w_ref, bs_ref, o_ref):
    t = u_ref.shape[0]
    u = _gelu(u_ref[...].astype(jnp.float32))
    v = _gelu(v_ref[...].astype(jnp.float32))
    vn = _ln(v, g_ref[...], b_ref[...]).astype(jnp.bfloat16)
    row = lax.broadcasted_iota(jnp.int32, (WIN, WIN), 0)
    col = lax.broadcasted_iota(jnp.int32, (WIN, WIN), 1)
    causal = row >= col
    for g in range(SGU_G):
        wm = jnp.where(causal, w_ref[g], 0.0).astype(jnp.bfloat16)
        bias = bs_ref[:, g:g + 1]
        for n in range(t // WIN):
            r0, c0 = n * WIN, g * 128
            mixed = jnp.dot(wm, vn[r0:r0 + WIN, c0:c0 + 128],
                            preferred_element_type=jnp.float32) + bias
            o_ref[r0:r0 + WIN, c0:c0 + 128] = (u[r0:r0 + WIN, c0:c0 + 128] * mixed
                                               ).astype(o_ref.dtype)


def _sgu(z, ln_g, ln_b, w_s, b_s):
    n = z.shape[0]
    t = 256
    ublk = 3 * ATT_W // SGU_W
    return pl.pallas_call(
        _sgu_body,
        grid=(n // t,),
        in_specs=[pl.BlockSpec((t, SGU_W), lambda i: (i, ublk)),
                  pl.BlockSpec((t, SGU_W), lambda i: (i, ublk + 1)),
                  pl.BlockSpec((1, SGU_W), lambda i: (0, 0)),
                  pl.BlockSpec((1, SGU_W), lambda i: (0, 0)),
                  pl.BlockSpec((SGU_G, WIN, WIN), lambda i: (0, 0, 0)),
                  pl.BlockSpec((WIN, SGU_G), lambda i: (0, 0))],
        out_specs=pl.BlockSpec((t, SGU_W), lambda i: (i, 0)),
        out_shape=jax.ShapeDtypeStruct((n, SGU_W), jnp.bfloat16),
        compiler_params=_cp(("arbitrary",)),
        name="sgu",
    )(z, z, ln_g.reshape(1, SGU_W), ln_b.reshape(1, SGU_W), w_s, b_s.T)


def _top2_of4(a, b, c, d):
    m1, n1 = jnp.maximum(a, b), jnp.minimum(a, b)
    m2, n2 = jnp.maximum(c, d), jnp.minimum(c, d)
    top = jnp.maximum(m1, m2)
    second = jnp.maximum(jnp.minimum(m1, m2), jnp.maximum(n1, n2))
    return top + second


def _route_body(l_ref, id_ref, gt_ref):
    rows = [l_ref[e:e + 1, :] for e in range(N_EXP)]
    mx = functools.reduce(jnp.maximum, rows)
    ex = [jnp.exp(r - mx) for r in rows]
    den = functools.reduce(lambda a, b: a + b, ex)
    p = [e / den for e in ex]
    score = [_top2_of4(*p[EPG * g:EPG * (g + 1)]) for g in range(N_GRP)]
    best, sel = score[0], jnp.zeros_like(score[0], dtype=jnp.int32)
    for g in range(1, N_GRP):
        better = score[g] > best
        best = jnp.where(better, score[g], best)
        sel = jnp.where(better, g, sel)
    cand = []
    for k in range(EPG):
        val = p[k]
        for g in range(1, N_GRP):
            val = jnp.where(sel == g, p[EPG * g + k], val)
        cand.append(val)
    p1, i1 = cand[0], jnp.zeros_like(sel)
    for k in range(1, EPG):
        better = cand[k] > p1
        p1 = jnp.where(better, cand[k], p1)
        i1 = jnp.where(better, k, i1)
    p2, i2 = jnp.full_like(p1, -1.0), jnp.zeros_like(sel)
    for k in range(EPG):
        better = (cand[k] > p2) & (i1 != k)
        p2 = jnp.where(better, cand[k], p2)
        i2 = jnp.where(better, k, i2)
    tot = p1 + p2
    id_ref[...] = jnp.concatenate([sel * EPG + i1, sel * EPG + i2], axis=0)
    pad = jnp.zeros((SUBLANES - 2, p1.shape[1]), jnp.float32)
    gt_ref[...] = jnp.concatenate([p1 / tot, p2 / tot, pad], axis=0)


def _route(logits):
    n = logits.shape[1]
    return pl.pallas_call(
        _route_body,
        out_shape=[jax.ShapeDtypeStruct((2, n), jnp.int32),
                   jax.ShapeDtypeStruct((SUBLANES, n), jnp.float32)],
        compiler_params=_cp(None),
        name="route",
    )(logits)


def _outproj_body(ya_ref, ys_ref, w_ref, h_ref, gin_ref, bin_ref, g_ref, b_ref,
                  rw2_ref, rb_ref, hf_ref, hp_ref, lg_ref, *, raw_input):
    mix = jnp.dot(ya_ref[...], w_ref[:ATT_W, :], preferred_element_type=jnp.float32)
    mix = mix + jnp.dot(ys_ref[...], w_ref[ATT_W:, :], preferred_element_type=jnp.float32)
    h = h_ref[...]
    if raw_input:
        h = _ln(h, gin_ref[...], bin_ref[...])
    h1 = _ln(ALPHA * h + mix, g_ref[...], b_ref[...])
    hf_ref[...] = h1
    _pack_rows(h1, hp_ref)
    hh = h1.astype(jnp.bfloat16)
    hl = (h1 - hh.astype(jnp.float32)).astype(jnp.bfloat16)
    nt = (((1,), (1,)), ((), ()))
    both = lax.dot_general(rw2_ref[...], hh, nt, preferred_element_type=jnp.float32)
    low = lax.dot_general(rw2_ref[:N_EXP, :], hl, nt, preferred_element_type=jnp.float32)
    lg_ref[...] = both[:N_EXP] + low + both[N_EXP:] + rb_ref[...]


def _outproj(ya, ys, wb, resid, gin, bin_, g, b, rw2, rb, raw_input):
    n = ya.shape[0]
    t = 512
    return pl.pallas_call(
        functools.partial(_outproj_body, raw_input=raw_input),
        grid=(n // t,),
        in_specs=[pl.BlockSpec((t, ATT_W), lambda i: (i, 0)),
                  pl.BlockSpec((t, SGU_W), lambda i: (i, 0)),
                  pl.BlockSpec((D, D), lambda i: (0, 0)),
                  pl.BlockSpec((t, D), lambda i: (i, 0)),
                  pl.BlockSpec((1, D), lambda i: (0, 0)),
                  pl.BlockSpec((1, D), lambda i: (0, 0)),
                  pl.BlockSpec((1, D), lambda i: (0, 0)),
                  pl.BlockSpec((1, D), lambda i: (0, 0)),
                  pl.BlockSpec((2 * N_EXP, D), lambda i: (0, 0)),
                  pl.BlockSpec((N_EXP, 1), lambda i: (0, 0))],
        out_specs=[pl.BlockSpec((t, D), lambda i: (i, 0)),
                   pl.BlockSpec((t * PITCH, LANES), lambda i: (i, 0)),
                   pl.BlockSpec((N_EXP, t), lambda i: (0, i))],
        out_shape=[jax.ShapeDtypeStruct((n, D), jnp.float32),
                   jax.ShapeDtypeStruct((n * PITCH, LANES), jnp.float32),
                   jax.ShapeDtypeStruct((N_EXP, n), jnp.float32)],
        compiler_params=_cp(("arbitrary",)),
        name="out_proj_ln_route",
    )(ya, ys, wb, resid, gin.reshape(1, D), bin_.reshape(1, D), g.reshape(1, D),
      b.reshape(1, D), rw2, rb)


def _weight_copies(pairs, layer, e, sem):
    copies = []
    for w_ref, stage, _ in pairs:
        rows = stage.shape[0] // W_CHUNKS
        for c in range(W_CHUNKS):
            copies.append(pltpu.make_async_copy(w_ref.at[layer, e, pl.ds(c * rows, rows)],
                                                stage.at[pl.ds(c * rows, rows)], sem))
    return copies


def _refresh_weights(pairs, layer, te_ref, tend_ref, nt, t, sem):
    e = te_ref[t]

    @pl.when((t == 0) | (e != te_ref[jnp.maximum(t - 1, 0)]))
    def _():
        for cp in _weight_copies(pairs, layer, e, sem):
            cp.wait()
        for _, stage, work in pairs:
            def cast(c, carry, stage=stage, work=work):
                rows = pl.ds(pl.multiple_of(c * CAST_ROWS, CAST_ROWS), CAST_ROWS)
                work[rows, :] = stage[rows, :].astype(jnp.bfloat16)
                return carry

            lax.fori_loop(0, stage.shape[0] // CAST_ROWS, cast, 0)
        nxt = tend_ref[e]

        @pl.when(nxt < nt)
        def _():
            e_next = te_ref[jnp.minimum(nxt, te_ref.shape[0] - 1)]
            for cp in _weight_copies(pairs, layer, e_next, sem):
                cp.start(priority=W_PRIORITY)


def _experts_body(te_ref, nt_ref, tend_ref, nvalid_ref, pos_ref,
                  hp_ref, wg_ref, wu_ref, wd_ref, y_ref,
                  xa, xb, ya, yb, wgs, wus, wds, wgb, wub, wdb, asg_ref, sems, *, layer, n):
    step = pl.program_id(0)
    nt = nt_ref[0]
    gsem = (sems.at[0], sems.at[1])
    ssem = (sems.at[2], sems.at[3])
    wsem = sems.at[4]
    pairs = ((wg_ref, wgs, wgb), (wu_ref, wus, wub), (wd_ref, wds, wdb))
    tile_rows = TM * PITCH

    def assignment(tile, r):
        valid = r < nvalid_ref[tile]
        return valid, asg_ref[jnp.where(valid, tile * TM + r, 0)]

    def gather_copy(tile, r, xbuf, sem):
        valid, a = assignment(tile, r)
        tok = jnp.where(valid, jnp.where(a >= n, a - n, a), 0)
        return pltpu.make_async_copy(hp_ref.at[pl.ds(tok * PITCH, PITCH)],
                                     xbuf.at[pl.ds(r * PITCH, PITCH)], sem)

    def scatter_copy(tile, r, ybuf, sem, dump_block):
        valid, a = assignment(jnp.maximum(tile, 0), r)
        dst = jnp.where(valid & (tile >= 0), a, 2 * n + dump_block * TM + r)
        return pltpu.make_async_copy(ybuf.at[pl.ds(r * PITCH, PITCH)],
                                     y_ref.at[pl.ds(dst * PITCH, PITCH)], sem)

    def wait_gather(xbuf, sem):
        pltpu.make_async_copy(hp_ref.at[pl.ds(0, tile_rows)], xbuf, sem).wait()

    def wait_scatter(ybuf, sem):
        pltpu.make_async_copy(ybuf, y_ref.at[pl.ds(0, tile_rows)], sem).wait()

    @pl.when(step == 0)
    def _():
        for cp in _weight_copies(pairs, layer, te_ref[0], wsem):
            cp.start(priority=W_PRIORITY)

        def invert(a, carry):
            asg_ref[pos_ref[a]] = a
            return carry

        lax.fori_loop(0, 2 * n, invert, 0, unroll=8)

        def issue(r, carry):
            gather_copy(0, r, xa, gsem[0]).start()
            return carry

        lax.fori_loop(0, TM, issue, 0)
        ya[...] = jnp.zeros_like(ya)
        yb[...] = jnp.zeros_like(yb)
        pltpu.make_async_copy(ya, y_ref.at[pl.ds(2 * n * PITCH, tile_rows)], ssem[0]).start()

    def phase(t, xcur, gcur, xnext, gnext, ycur, scur, yprev, sprev, cur_block):
        @pl.when(t < nt)
        def _():
            _refresh_weights(pairs, layer, te_ref, tend_ref, nt, t, wsem)
            wait_gather(xcur, gcur)
            tnext = jnp.minimum(t + 1, nt - 1)
            for r in range(TM):
                gather_copy(tnext, r, xnext, gnext).start()
            for r in range(TM):
                scatter_copy(t - 1, r, yprev, sprev, 1 - cur_block).start()
            x = jnp.concatenate([c.astype(jnp.bfloat16) for c in _unpack_rows(xcur, TM)],
                                axis=1)
            g = jnp.dot(x, wgb[...], preferred_element_type=jnp.float32)
            u = jnp.dot(x, wub[...], preferred_element_type=jnp.float32)
            hid = (g * jax.nn.sigmoid(g) * u).astype(jnp.bfloat16)
            y = jnp.dot(hid, wdb[...], preferred_element_type=jnp.float32)
            wait_scatter(ycur, scur)
            _pack_rows(y, ycur)

            @pl.when(t == nt - 1)
            def _():
                wait_gather(xnext, gnext)
                wait_scatter(yprev, sprev)

                def issue(r, carry):
                    scatter_copy(t, r, ycur, scur, cur_block).start()
                    return carry

                lax.fori_loop(0, TM, issue, 0)
                wait_scatter(ycur, scur)

    phase(2 * step, xa, gsem[0], xb, gsem[1], ya, ssem[0], yb, ssem[1], 0)
    phase(2 * step + 1, xb, gsem[1], xa, gsem[0], yb, ssem[1], ya, ssem[0], 1)


def _experts(tile_expert, ntiles, tile_end, nvalid, pos, hp, w_gate, w_up, w_down,
             layer, n, max_tiles):
    any_spec = pl.BlockSpec(memory_space=pl.ANY)
    slab = pltpu.VMEM((TM * PITCH, LANES), jnp.float32)
    return pl.pallas_call(
        functools.partial(_experts_body, layer=layer, n=n),
        grid_spec=pltpu.PrefetchScalarGridSpec(
            num_scalar_prefetch=5,
            grid=(max_tiles // 2,),
            in_specs=[any_spec, any_spec, any_spec, any_spec],
            out_specs=any_spec,
            scratch_shapes=[slab, slab, slab, slab,
                            pltpu.VMEM((D, F), jnp.float32),
                            pltpu.VMEM((D, F), jnp.float32),
                            pltpu.VMEM((F, D), jnp.float32),
                            pltpu.VMEM((D, F), jnp.bfloat16),
                            pltpu.VMEM((D, F), jnp.bfloat16),
                            pltpu.VMEM((F, D), jnp.bfloat16),
                            pltpu.SMEM((max_tiles * TM,), jnp.int32),
                            pltpu.SemaphoreType.DMA((5,))]),
        out_shape=jax.ShapeDtypeStruct(((2 * n + 2 * TM) * PITCH, LANES), jnp.float32),
        compiler_params=_cp(("arbitrary",), disable_bounds_checks=True),
        name="experts",
    )(tile_expert, ntiles, tile_end, nvalid, pos, hp, w_gate, w_up, w_down)


def _combine_body(y0_ref, y1_ref, h_ref, gt_ref, g_ref, b_ref, hf_ref, *maybe_hb_ref):
    t = h_ref.shape[0]
    y0 = jnp.concatenate(_unpack_rows(y0_ref, t), axis=1)
    y1 = jnp.concatenate(_unpack_rows(y1_ref, t), axis=1)
    gt = jnp.transpose(gt_ref[...])
    ffn = gt[:, 0:1] * y0 + gt[:, 1:2] * y1
    h2 = _ln(ALPHA * h_ref[...] + ffn, g_ref[...], b_ref[...])
    hf_ref[...] = h2
    for hb_ref in maybe_hb_ref:
        hb_ref[...] = h2.astype(jnp.bfloat16)


def _combine(yg, hf, gates, g, b, emit_bf16):
    n = hf.shape[0]
    t = 256
    nb = n // t
    row_spec = pl.BlockSpec((t, D), lambda i: (i, 0))
    out_specs = [row_spec] + ([row_spec] if emit_bf16 else [])
    out_shape = [jax.ShapeDtypeStruct((n, D), jnp.float32)]
    if emit_bf16:
        out_shape.append(jax.ShapeDtypeStruct((n, D), jnp.bfloat16))
    return pl.pallas_call(
        _combine_body,
        grid=(nb,),
        in_specs=[pl.BlockSpec((t * PITCH, LANES), lambda i: (i, 0)),
                  pl.BlockSpec((t * PITCH, LANES), lambda i: (nb + i, 0)),
                  row_spec,
                  pl.BlockSpec((SUBLANES, t), lambda i: (0, i)),
                  pl.BlockSpec((1, D), lambda i: (0, 0)),
                  pl.BlockSpec((1, D), lambda i: (0, 0))],
        out_specs=out_specs,
        out_shape=out_shape,
        compiler_params=_cp(("arbitrary",)),
        name="combine_ln",
    )(yg, yg, hf, gates, g.reshape(1, D), b.reshape(1, D))


def _dispatch_plan(ids, n, max_tiles):
    i32 = jnp.int32
    e_flat = ids.reshape(-1)
    experts = jnp.arange(N_EXP, dtype=i32)
    onehot = (e_flat[:, None] == experts[None, :]).astype(i32)
    csum = jnp.cumsum(onehot, axis=0)
    counts = csum[-1]
    tiles = (counts + TM - 1) // TM
    tile_end = jnp.cumsum(tiles).astype(i32)
    tile_start = tile_end - tiles
    pos = jnp.sum(onehot * (csum - 1 + (tile_start * TM)[None, :]), axis=1).astype(i32)
    tile_ids = jnp.arange(max_tiles, dtype=i32)
    tile_expert = jnp.minimum(
        jnp.sum((tile_ids[:, None] >= tile_end[None, :]).astype(i32), axis=1), N_EXP - 1)
    mine = (tile_expert[:, None] == experts[None, :]).astype(i32)
    before = (tile_ids - jnp.sum(mine * tile_start[None, :], axis=1)) * TM
    nvalid = jnp.clip(jnp.sum(mine * counts[None, :], axis=1) - before, 0, TM).astype(i32)
    return pos, nvalid, tile_expert, tile_end[-1:], tile_end


def kernel(x, ln_in_g, ln_in_b, w_in, rel_bias, sgu_ln_g, sgu_ln_b, sgu_w, sgu_b, w_out,
           ln1_g, ln1_b, router_w, router_b, w_gate, w_up, w_down, ln2_g, ln2_b):
    batch, seq, _ = x.shape
    n = batch * seq
    max_tiles = (2 * n) // TM + N_EXP
    rwt = router_w.T
    rwh = rwt.astype(jnp.bfloat16)
    rwl = (rwt - rwh.astype(jnp.float32)).astype(jnp.bfloat16)
    rw2 = jnp.concatenate([rwh, rwl], axis=0)
    rb = router_b.reshape(N_EXP, 1)

    hf = x.reshape(n, D)
    hb = _ln0(hf, ln_in_g, ln_in_b)
    for l in range(DEPTH):
        z = _inproj(hb, w_in, l)
        ya = _attention(z, _attn_bias(rel_bias[l]), batch, seq)
        ys = _sgu(z, sgu_ln_g[l], sgu_ln_b[l], sgu_w[l], sgu_b[l])
        hf, hp, logits = _outproj(ya, ys, w_out[l].astype(jnp.bfloat16), hf,
                                  ln_in_g, ln_in_b, ln1_g[l], ln1_b[l], rw2, rb,
                                  raw_input=(l == 0))
        ids, gates = _route(logits)
        pos, nvalid, tile_expert, ntiles, tile_end = _dispatch_plan(ids, n, max_tiles)
        yslots = _experts(tile_expert, ntiles, tile_end, nvalid, pos, hp,
                          w_gate, w_up, w_down, l, n, max_tiles)
        outs = _combine(yslots, hf, gates, ln2_g[l], ln2_b[l], emit_bf16=(l + 1 < DEPTH))
        hf = outs[0]
        hb = outs[1] if l + 1 < DEPTH else None
    return hf.reshape(batch, seq, D)
```

```python
import functools
import math

import jax
import jax.numpy as jnp
import numpy as np
from jax import lax
from jax.experimental import pallas as pl
from jax.experimental.pallas import tpu as pltpu

D = 2048
HEADS = 8
DH = 128
ATT_W = HEADS * DH
SGU_G = 8
SGU_W = SGU_G * 128
WIN = 128
IN_W = 3 * ATT_W + 2 * SGU_W
CHUNK = 64
N_PREV = 8
REL_CLIP = 128
N_EXP = 16
N_GRP = 4
EPG = 4
F = 1024
DEPTH = 2
ALPHA = (2.0 * DEPTH) ** 0.25
EPS = 1e-5
NEG = -1e30
LOG2E = math.log2(math.e)

LANES = 128
SUBLANES = 8
VMEM_LIMIT = 56 * 1024 * 1024

QB = 256
KWIN = QB + N_PREV * CHUNK
TM = 256
WPT = D // LANES
W_CHUNKS = 4
CAST_ROWS = 128
PITCH = WPT + 1
W_PRIORITY = 1


def _cp(sem, vmem=VMEM_LIMIT, **kw):
    return pltpu.CompilerParams(dimension_semantics=sem, vmem_limit_bytes=vmem, **kw)


def _ln(x, g, b):
    mu = jnp.mean(x, axis=-1, keepdims=True)
    xc = x - mu
    var = jnp.mean(xc * xc, axis=-1, keepdims=True)
    return xc * lax.rsqrt(var + EPS) * g + b


def _pack_rows(y, o_ref, row0=0):
    t = y.shape[0]
    base = row0 * PITCH
    for s in range(WPT):
        o_ref[pl.ds(base + s, t, stride=PITCH), :] = y[:, LANES * s:LANES * (s + 1)]
    for s in range(WPT, PITCH):
        o_ref[pl.ds(base + s, t, stride=PITCH), :] = jnp.zeros((t, LANES), jnp.float32)


def _unpack_rows(p_ref, t):
    return [p_ref[pl.ds(s, t, stride=PITCH), :] for s in range(WPT)]


def _ln0_body(x_ref, g_ref, b_ref, hb_ref):
    hb_ref[...] = _ln(x_ref[...], g_ref[...], b_ref[...]).astype(jnp.bfloat16)


def _ln0(x, g, b):
    n = x.shape[0]
    t = 256
    return pl.pallas_call(
        _ln0_body,
        grid=(n // t,),
        in_specs=[pl.BlockSpec((t, D), lambda i: (i, 0)),
                  pl.BlockSpec((1, D), lambda i: (0, 0)),
                  pl.BlockSpec((1, D), lambda i: (0, 0))],
        out_specs=pl.BlockSpec((t, D), lambda i: (i, 0)),
        out_shape=jax.ShapeDtypeStruct((n, D), jnp.bfloat16),
        compiler_params=_cp(("arbitrary",)),
        name="ln_in",
    )(x, g.reshape(1, D), b.reshape(1, D))


def _gelu(x):
    return 0.5 * x * (1.0 + lax.erf(x * (1.0 / math.sqrt(2.0))))


def _inproj_body(x_ref, w_ref, o_ref, wb_ref, *, gelu_from):
    j = pl.program_id(0)

    @pl.when(pl.program_id(1) == 0)
    def _():
        wb_ref[...] = w_ref[0].astype(jnp.bfloat16)

    @pl.when(j < gelu_from)
    def _():
        o_ref[...] = jnp.dot(x_ref[...], wb_ref[...],
                             preferred_element_type=jnp.float32).astype(o_ref.dtype)

    @pl.when(j >= gelu_from)
    def _():
        o_ref[...] = _gelu(jnp.dot(x_ref[...], wb_ref[...],
                                   preferred_element_type=jnp.float32)).astype(o_ref.dtype)


def _inproj(hb, w, layer):
    n = hb.shape[0]
    tm, tn = 1024, 1024
    return pl.pallas_call(
        functools.partial(_inproj_body, gelu_from=3 * ATT_W // tn),
        grid=(IN_W // tn, n // tm),
        in_specs=[pl.BlockSpec((tm, D), lambda j, i: (i, 0)),
                  pl.BlockSpec((1, D, tn), lambda j, i: (layer, 0, j))],
        out_specs=pl.BlockSpec((tm, tn), lambda j, i: (i, j)),
        out_shape=jax.ShapeDtypeStruct((n, IN_W), jnp.bfloat16),
        scratch_shapes=[pltpu.VMEM((D, tn), jnp.bfloat16)],
        compiler_params=_cp(("arbitrary", "arbitrary")),
        name="in_proj",
    )(hb, w)


def _attn_body(q_ref, k_ref, v_ref, bm_ref, o_ref):
    seq = q_ref.shape[0]
    scale = LOG2E / math.sqrt(DH)
    for j in range(seq // QB):
        q0 = j * QB
        k0 = max(0, q0 - N_PREV * CHUNK)
        nk = q0 + QB - k0
        q = q_ref[q0:q0 + QB, :]
        k = k_ref[k0:k0 + nk, :]
        v = v_ref[k0:k0 + nk, :]
        s = lax.dot_general(q, k, (((1,), (1,)), ((), ())),
                            preferred_element_type=jnp.float32)
        s = s * scale + bm_ref[0, :, KWIN - nk:]
        m = jnp.max(s, axis=-1, keepdims=True)
        p = jnp.exp2(s - m)
        l = jnp.sum(p, axis=-1, keepdims=True)
        o = jnp.dot(p.astype(jnp.bfloat16), v, preferred_element_type=jnp.float32)
        o_ref[q0:q0 + QB, :] = (o / l).astype(o_ref.dtype)


def _bias_body(g_ref, o_ref):
    period = g_ref.shape[2]
    rows = jnp.broadcast_to(g_ref[0], (QB, period))
    toeplitz = pltpu.roll(rows, 0, 1, stride=1, stride_axis=0)[:, :KWIN]
    qc = lax.broadcasted_iota(jnp.int32, (QB, KWIN), 0) // CHUNK
    kc = lax.broadcasted_iota(jnp.int32, (QB, KWIN), 1) // CHUNK
    band = (kc >= qc) & (kc <= qc + N_PREV)
    o_ref[0] = jnp.where(band, toeplitz, NEG)


def _attn_bias(rel_table):
    period = KWIN + QB
    off = N_PREV * CHUNK
    tab = rel_table.astype(jnp.float32) * LOG2E
    heads = tab.shape[0]
    far = tab[:, 2 * REL_CLIP:]
    n_far = off - REL_CLIP + 1
    near = tab[:, 1:2 * REL_CLIP][:, ::-1]
    n_low = KWIN - n_far - near.shape[1]
    g = jnp.concatenate([jnp.broadcast_to(far, (heads, n_far)), near,
                         jnp.broadcast_to(tab[:, :1], (heads, n_low)),
                         jnp.broadcast_to(far, (heads, period - KWIN))], axis=1)
    return pl.pallas_call(
        _bias_body,
        grid=(heads,),
        in_specs=[pl.BlockSpec((1, 1, period), lambda h: (h, 0, 0))],
        out_specs=pl.BlockSpec((1, QB, KWIN), lambda h: (h, 0, 0)),
        out_shape=jax.ShapeDtypeStruct((heads, QB, KWIN), jnp.float32),
        compiler_params=_cp(("arbitrary",)),
        name="attn_bias",
    )(g.reshape(heads, 1, period))


def _attention(z, bm, batch, seq):
    return pl.pallas_call(
        _attn_body,
        grid=(batch, HEADS),
        in_specs=[pl.BlockSpec((seq, DH), lambda b, h: (b, h)),
                  pl.BlockSpec((seq, DH), lambda b, h: (b, HEADS + h)),
                  pl.BlockSpec((seq, DH), lambda b, h: (b, 2 * HEADS + h)),
                  pl.BlockSpec((1, QB, KWIN), lambda b, h: (h, 0, 0))],
        out_specs=pl.BlockSpec((seq, DH), lambda b, h: (b, h)),
        out_shape=jax.ShapeDtypeStruct((batch * seq, ATT_W), jnp.bfloat16),
        compiler_params=_cp(("arbitrary", "arbitrary")),
        name="band_attn",
    )(z, z, z, bm)


def _sgu_body(u_ref, v_ref, g_ref, b_ref, w_ref, bs_ref, o_ref):
    t = u_ref.shape[0]
    u = u_ref[...].astype(jnp.float32)
    vn = _ln(v_ref[...].astype(jnp.float32), g_ref[...], b_ref[...]).astype(jnp.bfloat16)
    row = lax.broadcasted_iota(jnp.int32, (WIN, WIN), 0)
    col = lax.broadcasted_iota(jnp.int32, (WIN, WIN), 1)
    causal = row >= col
    for g in range(SGU_G):
        wm = jnp.where(causal, w_ref[g], 0.0).astype(jnp.bfloat16)
        bias = bs_ref[:, g:g + 1]
        for n in range(t // WIN):
            r0, c0 = n * WIN, g * 128
            mixed = jnp.dot(wm, vn[r0:r0 + WIN, c0:c0 + 128],
                            preferred_element_type=jnp.float32) + bias
            o_ref[r0:r0 + WIN, c0:c0 + 128] = (u[r0:r0 + WIN, c0:c0 + 128] * mixed
                                               ).astype(o_ref.dtype)


def _sgu(z, ln_g, ln_b, w_s, b_s):
    n = z.shape[0]
    t = 256
    ublk = 3 * ATT_W // SGU_W
    return pl.pallas_call(
        _sgu_body,
        grid=(n // t,),
        in_specs=[pl.BlockSpec((t, SGU_W), lambda i: (i, ublk)),
                  pl.BlockSpec((t, SGU_W), lambda i: (i, ublk + 1)),
                  pl.BlockSpec((1, SGU_W), lambda i: (0, 0)),
                  pl.BlockSpec((1, SGU_W), lambda i: (0, 0)),
                  pl.BlockSpec((SGU_G, WIN, WIN), lambda i: (0, 0, 0)),
                  pl.BlockSpec((WIN, SGU_G), lambda i: (0, 0))],
        out_specs=pl.BlockSpec((t, SGU_W), lambda i: (i, 0)),
        out_shape=jax.ShapeDtypeStruct((n, SGU_W), jnp.bfloat16),
        compiler_params=_cp(("arbitrary",)),
        name="sgu",
    )(z, z, ln_g.reshape(1, SGU_W), ln_b.reshape(1, SGU_W), w_s, b_s.T)


def _top2_of4(a, b, c, d):
    m1, n1 = jnp.maximum(a, b), jnp.minimum(a, b)
    m2, n2 = jnp.maximum(c, d), jnp.minimum(c, d)
    top = jnp.maximum(m1, m2)
    second = jnp.maximum(jnp.minimum(m1, m2), jnp.maximum(n1, n2))
    return top + second


def _route_body(l_ref, id_ref, gt_ref):
    rows = [l_ref[e:e + 1, :] for e in range(N_EXP)]
    mx = functools.reduce(jnp.maximum, rows)
    ex = [jnp.exp(r - mx) for r in rows]
    den = functools.reduce(lambda a, b: a + b, ex)
    p = [e / den for e in ex]
    score = [_top2_of4(*p[EPG * g:EPG * (g + 1)]) for g in range(N_GRP)]
    best, sel = score[0], jnp.zeros_like(score[0], dtype=jnp.int32)
    for g in range(1, N_GRP):
        better = score[g] > best
        best = jnp.where(better, score[g], best)
        sel = jnp.where(better, g, sel)
    cand = []
    for k in range(EPG):
        val = p[k]
        for g in range(1, N_GRP):
            val = jnp.where(sel == g, p[EPG * g + k], val)
        cand.append(val)
    p1, i1 = cand[0], jnp.zeros_like(sel)
    for k in range(1, EPG):
        better = cand[k] > p1
        p1 = jnp.where(better, cand[k], p1)
        i1 = jnp.where(better, k, i1)
    p2, i2 = jnp.full_like(p1, -1.0), jnp.zeros_like(sel)
    for k in range(EPG):
        better = (cand[k] > p2) & (i1 != k)
        p2 = jnp.where(better, cand[k], p2)
        i2 = jnp.where(better, k, i2)
    tot = p1 + p2
    id_ref[...] = jnp.concatenate([sel * EPG + i1, sel * EPG + i2], axis=0)
    pad = jnp.zeros((SUBLANES - 2, p1.shape[1]), jnp.float32)
    gt_ref[...] = jnp.concatenate([p1 / tot, p2 / tot, pad], axis=0)


def _route(logits):
    n = logits.shape[1]
    return pl.pallas_call(
        _route_body,
        out_shape=[jax.ShapeDtypeStruct((2, n), jnp.int32),
                   jax.ShapeDtypeStruct((SUBLANES, n), jnp.float32)],
        compiler_params=_cp(None),
        name="route",
    )(logits)


def _outproj_body(ya_ref, ys_ref, w_ref, h_ref, gin_ref, bin_ref, g_ref, b_ref,
                  rw2_ref, rb_ref, hf_ref, hp_ref, lg_ref, *, raw_input):
    mix = jnp.dot(ya_ref[...], w_ref[:ATT_W, :], preferred_element_type=jnp.float32)
    mix = mix + jnp.dot(ys_ref[...], w_ref[ATT_W:, :], preferred_element_type=jnp.float32)
    h = h_ref[...]
    if raw_input:
        h = _ln(h, gin_ref[...], bin_ref[...])
    h1 = _ln(ALPHA * h + mix, g_ref[...], b_ref[...])
    hf_ref[...] = h1
    _pack_rows(h1, hp_ref)
    hh = h1.astype(jnp.bfloat16)
    hl = (h1 - hh.astype(jnp.float32)).astype(jnp.bfloat16)
    nt = (((1,), (1,)), ((), ()))
    both = lax.dot_general(rw2_ref[...], hh, nt, preferred_element_type=jnp.float32)
    low = lax.dot_general(rw2_ref[:N_EXP, :], hl, nt, preferred_element_type=jnp.float32)
    lg_ref[...] = both[:N_EXP] + low + both[N_EXP:] + rb_ref[...]


def _outproj(ya, ys, wb, resid, gin, bin_, g, b, rw2, rb, raw_input):
    n = ya.shape[0]
    t = 512
    return pl.pallas_call(
        functools.partial(_outproj_body, raw_input=raw_input),
        grid=(n // t,),
        in_specs=[pl.BlockSpec((t, ATT_W), lambda i: (i, 0)),
                  pl.BlockSpec((t, SGU_W), lambda i: (i, 0)),
                  pl.BlockSpec((D, D), lambda i: (0, 0)),
                  pl.BlockSpec((t, D), lambda i: (i, 0)),
                  pl.BlockSpec((1, D), lambda i: (0, 0)),
                  pl.BlockSpec((1, D), lambda i: (0, 0)),
                  pl.BlockSpec((1, D), lambda i: (0, 0)),
                  pl.BlockSpec((1, D), lambda i: (0, 0)),
                  pl.BlockSpec((2 * N_EXP, D), lambda i: (0, 0)),
                  pl.BlockSpec((N_EXP, 1), lambda i: (0, 0))],
        out_specs=[pl.BlockSpec((t, D), lambda i: (i, 0)),
                   pl.BlockSpec((t * PITCH, LANES), lambda i: (i, 0)),
                   pl.BlockSpec((N_EXP, t), lambda i: (0, i))],
        out_shape=[jax.ShapeDtypeStruct((n, D), jnp.float32),
                   jax.ShapeDtypeStruct((n * PITCH, LANES), jnp.float32),
                   jax.ShapeDtypeStruct((N_EXP, n), jnp.float32)],
        compiler_params=_cp(("arbitrary",)),
        name="out_proj_ln_route",
    )(ya, ys, wb, resid, gin.reshape(1, D), bin_.reshape(1, D), g.reshape(1, D),
      b.reshape(1, D), rw2, rb)


def _weight_copies(pairs, layer, e, sem):
    copies = []
    for w_ref, stage, _ in pairs:
        rows = stage.shape[0] // W_CHUNKS
        for c in range(W_CHUNKS):
            copies.append(pltpu.make_async_copy(w_ref.at[layer, e, pl.ds(c * rows, rows)],
                                                stage.at[pl.ds(c * rows, rows)], sem))
    return copies


def _refresh_weights(pairs, layer, te_ref, tend_ref, nt, t, sem):
    e = te_ref[t]

    @pl.when((t == 0) | (e != te_ref[jnp.maximum(t - 1, 0)]))
    def _():
        for cp in _weight_copies(pairs, layer, e, sem):
            cp.wait()
        for _, stage, work in pairs:
            def cast(c, carry, stage=stage, work=work):
                rows = pl.ds(pl.multiple_of(c * CAST_ROWS, CAST_ROWS), CAST_ROWS)
                work[rows, :] = stage[rows, :].astype(jnp.bfloat16)
                return carry

            lax.fori_loop(0, stage.shape[0] // CAST_ROWS, cast, 0)
        nxt = tend_ref[e]

        @pl.when(nxt < nt)
        def _():
            e_next = te_ref[jnp.minimum(nxt, te_ref.shape[0] - 1)]
            for cp in _weight_copies(pairs, layer, e_next, sem):
                cp.start(priority=W_PRIORITY)


def _experts_body(te_ref, nt_ref, tend_ref, nvalid_ref, pos_ref,
                  hp_ref, wg_ref, wu_ref, wd_ref, y_ref,
                  xa, xb, ya, yb, wgs, wus, wds, wgb, wub, wdb, asg_ref, sems, *, layer, n):
    step = pl.program_id(0)
    nt = nt_ref[0]
    gsem = (sems.at[0], sems.at[1])
    ssem = (sems.at[2], sems.at[3])
    wsem = sems.at[4]
    pairs = ((wg_ref, wgs, wgb), (wu_ref, wus, wub), (wd_ref, wds, wdb))
    tile_rows = TM * PITCH

    def assignment(tile, r):
        valid = r < nvalid_ref[tile]
        return valid, asg_ref[jnp.where(valid, tile * TM + r, 0)]

    def gather_copy(tile, r, xbuf, sem):
        valid, a = assignment(tile, r)
        tok = jnp.where(valid, jnp.where(a >= n, a - n, a), 0)
        return pltpu.make_async_copy(hp_ref.at[pl.ds(tok * PITCH, PITCH)],
                                     xbuf.at[pl.ds(r * PITCH, PITCH)], sem)

    def scatter_copy(tile, r, ybuf, sem, dump_block):
        valid, a = assignment(jnp.maximum(tile, 0), r)
        dst = jnp.where(valid & (tile >= 0), a, 2 * n + dump_block * TM + r)
        return pltpu.make_async_copy(ybuf.at[pl.ds(r * PITCH, PITCH)],
                                     y_ref.at[pl.ds(dst * PITCH, PITCH)], sem)

    def wait_gather(xbuf, sem):
        pltpu.make_async_copy(hp_ref.at[pl.ds(0, tile_rows)], xbuf, sem).wait()

    def wait_scatter(ybuf, sem):
        pltpu.make_async_copy(ybuf, y_ref.at[pl.ds(0, tile_rows)], sem).wait()

    @pl.when(step == 0)
    def _():
        for cp in _weight_copies(pairs, layer, te_ref[0], wsem):
            cp.start(priority=W_PRIORITY)

        def invert(a, carry):
            asg_ref[pos_ref[a]] = a
            return carry

        lax.fori_loop(0, 2 * n, invert, 0, unroll=32)

        def issue(r, carry):
            gather_copy(0, r, xa, gsem[0]).start()
            return carry

        lax.fori_loop(0, TM, issue, 0)
        ya[...] = jnp.zeros_like(ya)
        yb[...] = jnp.zeros_like(yb)
        pltpu.make_async_copy(ya, y_ref.at[pl.ds(2 * n * PITCH, tile_rows)], ssem[0]).start()

    def phase(t, xcur, gcur, xnext, gnext, ycur, scur, yprev, sprev, cur_block):
        @pl.when(t < nt)
        def _():
            _refresh_weights(pairs, layer, te_ref, tend_ref, nt, t, wsem)
            wait_gather(xcur, gcur)
            tnext = jnp.minimum(t + 1, nt - 1)
            for r in range(TM):
                gather_copy(tnext, r, xnext, gnext).start()
            for r in range(TM):
                scatter_copy(t - 1, r, yprev, sprev, 1 - cur_block).start()
            x = jnp.concatenate([c.astype(jnp.bfloat16) for c in _unpack_rows(xcur, TM)],
                                axis=1)
            g = jnp.dot(x, wgb[...], preferred_element_type=jnp.float32)
            u = jnp.dot(x, wub[...], preferred_element_type=jnp.float32)
            hid = (g * jax.nn.sigmoid(g) * u).astype(jnp.bfloat16)
            y = jnp.dot(hid, wdb[...], preferred_element_type=jnp.float32)
            wait_scatter(ycur, scur)
            _pack_rows(y, ycur)

            @pl.when(t == nt - 1)
            def _():
                wait_gather(xnext, gnext)
                wait_scatter(yprev, sprev)

                def issue(r, carry):
                    scatter_copy(t, r, ycur, scur, cur_block).start()
                    return carry

                lax.fori_loop(0, TM, issue, 0)
                wait_scatter(ycur, scur)

    phase(2 * step, xa, gsem[0], xb, gsem[1], ya, ssem[0], yb, ssem[1], 0)
    phase(2 * step + 1, xb, gsem[1], xa, gsem[0], yb, ssem[1], ya, ssem[0], 1)


def _experts(tile_expert, ntiles, tile_end, nvalid, pos, hp, w_gate, w_up, w_down,
             layer, n, max_tiles):
    any_spec = pl.BlockSpec(memory_space=pl.ANY)
    slab = pltpu.VMEM((TM * PITCH, LANES), jnp.float32)
    return pl.pallas_call(
        functools.partial(_experts_body, layer=layer, n=n),
        grid_spec=pltpu.PrefetchScalarGridSpec(
            num_scalar_prefetch=5,
            grid=(max_tiles // 2,),
            in_specs=[any_spec, any_spec, any_spec, any_spec],
            out_specs=any_spec,
            scratch_shapes=[slab, slab, slab, slab,
                            pltpu.VMEM((D, F), jnp.float32),
                            pltpu.VMEM((D, F), jnp.float32),
                            pltpu.VMEM((F, D), jnp.float32),
                            pltpu.VMEM((D, F), jnp.bfloat16),
                            pltpu.VMEM((D, F), jnp.bfloat16),
                            pltpu.VMEM((F, D), jnp.bfloat16),
                            pltpu.SMEM((max_tiles * TM,), jnp.int32),
                            pltpu.SemaphoreType.DMA((5,))]),
        out_shape=jax.ShapeDtypeStruct(((2 * n + 2 * TM) * PITCH, LANES), jnp.float32),
        compiler_params=_cp(("arbitrary",), disable_bounds_checks=True),
        name="experts",
    )(tile_expert, ntiles, tile_end, nvalid, pos, hp, w_gate, w_up, w_down)


def _combine_body(y0_ref, y1_ref, h_ref, gt_ref, g_ref, b_ref, hf_ref, *maybe_hb_ref):
    t = h_ref.shape[0]
    y0 = jnp.concatenate(_unpack_rows(y0_ref, t), axis=1)
    y1 = jnp.concatenate(_unpack_rows(y1_ref, t), axis=1)
    gt = jnp.transpose(gt_ref[...])
    ffn = gt[:, 0:1] * y0 + gt[:, 1:2] * y1
    h2 = _ln(ALPHA * h_ref[...] + ffn, g_ref[...], b_ref[...])
    hf_ref[...] = h2
    for hb_ref in maybe_hb_ref:
        hb_ref[...] = h2.astype(jnp.bfloat16)


def _combine(yg, hf, gates, g, b, emit_bf16):
    n = hf.shape[0]
    t = 256
    nb = n // t
    row_spec = pl.BlockSpec((t, D), lambda i: (i, 0))
    out_specs = [row_spec] + ([row_spec] if emit_bf16 else [])
    out_shape = [jax.ShapeDtypeStruct((n, D), jnp.float32)]
    if emit_bf16:
        out_shape.append(jax.ShapeDtypeStruct((n, D), jnp.bfloat16))
    return pl.pallas_call(
        _combine_body,
        grid=(nb,),
        in_specs=[pl.BlockSpec((t * PITCH, LANES), lambda i: (i, 0)),
                  pl.BlockSpec((t * PITCH, LANES), lambda i: (nb + i, 0)),
                  row_spec,
                  pl.BlockSpec((SUBLANES, t), lambda i: (0, i)),
                  pl.BlockSpec((1, D), lambda i: (0, 0)),
                  pl.BlockSpec((1, D), lambda i: (0, 0))],
        out_specs=out_specs,
        out_shape=out_shape,
        compiler_params=_cp(("arbitrary",)),
        name="combine_ln",
    )(yg, yg, hf, gates, g.reshape(1, D), b.reshape(1, D))


def _dispatch_plan(ids, n, max_tiles):
    i32 = jnp.int32
    e_flat = ids.reshape(-1)
    experts = jnp.arange(N_EXP, dtype=i32)
    onehot = (e_flat[:, None] == experts[None, :]).astype(i32)
    csum = jnp.cumsum(onehot, axis=0)
    counts = csum[-1]
    tiles = (counts + TM - 1) // TM
    tile_end = jnp.cumsum(tiles).astype(i32)
    tile_start = tile_end - tiles
    pos = jnp.sum(onehot * (csum - 1 + (tile_start * TM)[None, :]), axis=1).astype(i32)
    tile_ids = jnp.arange(max_tiles, dtype=i32)
    tile_expert = jnp.minimum(
        jnp.sum((tile_ids[:, None] >= tile_end[None, :]).astype(i32), axis=1), N_EXP - 1)
    mine = (tile_expert[:, None] == experts[None, :]).astype(i32)
    before = (tile_ids - jnp.sum(mine * tile_start[None, :], axis=1)) * TM
    nvalid = jnp.clip(jnp.sum(mine * counts[None, :], axis=1) - before, 0, TM).astype(i32)
    return pos, nvalid, tile_expert, tile_end[-1:], tile_end


def kernel(x, ln_in_g, ln_in_b, w_in, rel_bias, sgu_ln_g, sgu_ln_b, sgu_w, sgu_b, w_out,
           ln1_g, ln1_b, router_w, router_b, w_gate, w_up, w_down, ln2_g, ln2_b):
    batch, seq, _ = x.shape
    n = batch * seq
    max_tiles = (2 * n) // TM + N_EXP
    rwt = router_w.T
    rwh = rwt.astype(jnp.bfloat16)
    rwl = (rwt - rwh.astype(jnp.float32)).astype(jnp.bfloat16)
    rw2 = jnp.concatenate([rwh, rwl], axis=0)
    rb = router_b.reshape(N_EXP, 1)

    hf = x.reshape(n, D)
    hb = _ln0(hf, ln_in_g, ln_in_b)
    for l in range(DEPTH):
        z = _inproj(hb, w_in, l)
        ya = _attention(z, _attn_bias(rel_bias[l]), batch, seq)
        ys = _sgu(z, sgu_ln_g[l], sgu_ln_b[l], sgu_w[l], sgu_b[l])
        hf, hp, logits = _outproj(ya, ys, w_out[l].astype(jnp.bfloat16), hf,
                                  ln_in_g, ln_in_b, ln1_g[l], ln1_b[l], rw2, rb,
                                  raw_input=(l == 0))
        ids, gates = _route(logits)
        pos, nvalid, tile_expert, ntiles, tile_end = _dispatch_plan(ids, n, max_tiles)
        yslots = _experts(tile_expert, ntiles, tile_end, nvalid, pos, hp,
                          w_gate, w_up, w_down, l, n, max_tiles)
        outs = _combine(yslots, hf, gates, ln2_g[l], ln2_b[l], emit_bf16=(l + 1 < DEPTH))
        hf = outs[0]
        hb = outs[1] if l + 1 < DEPTH else None
    return hf.reshape(batch, seq, D)
```

```python
import functools
import math

import jax
import jax.numpy as jnp
import numpy as np
from jax import lax
from jax.experimental import pallas as pl
from jax.experimental.pallas import tpu as pltpu

D = 2048
HEADS = 8
DH = 128
ATT_W = HEADS * DH
SGU_G = 8
SGU_W = SGU_G * 128
WIN = 128
IN_W = 3 * ATT_W + 2 * SGU_W
CHUNK = 64
N_PREV = 8
REL_CLIP = 128
N_EXP = 16
N_GRP = 4
EPG = 4
F = 1024
DEPTH = 2
ALPHA = (2.0 * DEPTH) ** 0.25
EPS = 1e-5
NEG = -1e30
LOG2E = math.log2(math.e)

LANES = 128
SUBLANES = 8
VMEM_LIMIT = 56 * 1024 * 1024

QB = 256
KWIN = QB + N_PREV * CHUNK
TM = 256
WPT = D // LANES
W_CHUNKS = 4
CAST_ROWS = 128
PITCH = WPT + 1
W_PRIORITY = 1


def _cp(sem, vmem=VMEM_LIMIT, **kw):
    return pltpu.CompilerParams(dimension_semantics=sem, vmem_limit_bytes=vmem, **kw)


def _ln(x, g, b):
    mu = jnp.mean(x, axis=-1, keepdims=True)
    xc = x - mu
    var = jnp.mean(xc * xc, axis=-1, keepdims=True)
    return xc * lax.rsqrt(var + EPS) * g + b


def _pack_rows(y, o_ref, row0=0):
    t = y.shape[0]
    base = row0 * PITCH
    for s in range(WPT):
        o_ref[pl.ds(base + s, t, stride=PITCH), :] = y[:, LANES * s:LANES * (s + 1)]
    for s in range(WPT, PITCH):
        o_ref[pl.ds(base + s, t, stride=PITCH), :] = jnp.zeros((t, LANES), jnp.float32)


def _unpack_rows(p_ref, t):
    return [p_ref[pl.ds(s, t, stride=PITCH), :] for s in range(WPT)]


def _ln0_body(x_ref, g_ref, b_ref, hb_ref):
    hb_ref[...] = _ln(x_ref[...], g_ref[...], b_ref[...]).astype(jnp.bfloat16)


def _ln0(x, g, b):
    n = x.shape[0]
    t = 256
    return pl.pallas_call(
        _ln0_body,
        grid=(n // t,),
        in_specs=[pl.BlockSpec((t, D), lambda i: (i, 0)),
                  pl.BlockSpec((1, D), lambda i: (0, 0)),
                  pl.BlockSpec((1, D), lambda i: (0, 0))],
        out_specs=pl.BlockSpec((t, D), lambda i: (i, 0)),
        out_shape=jax.ShapeDtypeStruct((n, D), jnp.bfloat16),
        compiler_params=_cp(("arbitrary",)),
        name="ln_in",
    )(x, g.reshape(1, D), b.reshape(1, D))


def _gelu(x):
    return 0.5 * x * (1.0 + lax.erf(x * (1.0 / math.sqrt(2.0))))


def _inproj_body(x_ref, w_ref, o_ref, wb_ref, *, gelu_from):
    j = pl.program_id(0)

    @pl.when(pl.program_id(1) == 0)
    def _():
        wb_ref[...] = w_ref[0].astype(jnp.bfloat16)

    @pl.when(j < gelu_from)
    def _():
        o_ref[...] = jnp.dot(x_ref[...], wb_ref[...],
                             preferred_element_type=jnp.float32).astype(o_ref.dtype)

    @pl.when(j >= gelu_from)
    def _():
        o_ref[...] = _gelu(jnp.dot(x_ref[...], wb_ref[...],
                                   preferred_element_type=jnp.float32)).astype(o_ref.dtype)


def _inproj(hb, w, layer):
    n = hb.shape[0]
    tm, tn = 1024, 1024
    return pl.pallas_call(
        functools.partial(_inproj_body, gelu_from=3 * ATT_W // tn),
        grid=(IN_W // tn, n // tm),
        in_specs=[pl.BlockSpec((tm, D), lambda j, i: (i, 0)),
                  pl.BlockSpec((1, D, tn), lambda j, i: (layer, 0, j))],
        out_specs=pl.BlockSpec((tm, tn), lambda j, i: (i, j)),
        out_shape=jax.ShapeDtypeStruct((n, IN_W), jnp.bfloat16),
        scratch_shapes=[pltpu.VMEM((D, tn), jnp.bfloat16)],
        compiler_params=_cp(("arbitrary", "arbitrary")),
        name="in_proj",
    )(hb, w)


def _attn_body(q_ref, k_ref, v_ref, bm_ref, o_ref):
    seq = q_ref.shape[0]
    scale = LOG2E / math.sqrt(DH)
    def window(j):
        q0 = j * QB
        k0 = max(0, q0 - N_PREV * CHUNK)
        return q0, k0, q0 + QB - k0

    def scores(j):
        q0, k0, nk = window(j)
        return lax.dot_general(q_ref[q0:q0 + QB, :], k_ref[k0:k0 + nk, :],
                               (((1,), (1,)), ((), ())), preferred_element_type=jnp.float32)

    def finish(j, p, l):
        q0, k0, nk = window(j)
        o = jnp.dot(p, v_ref[k0:k0 + nk, :], preferred_element_type=jnp.float32)
        o_ref[q0:q0 + QB, :] = (o / l).astype(o_ref.dtype)

    blocks = seq // QB
    s_next = scores(0)
    pending = None
    for j in range(blocks):
        _, _, nk = window(j)
        s = s_next
        if j + 1 < blocks:
            s_next = scores(j + 1)
        if pending is not None:
            finish(*pending)
        s = s * scale + bm_ref[0, :, KWIN - nk:]
        m = jnp.max(s, axis=-1, keepdims=True)
        p = jnp.exp2(s - m)
        pending = (j, p.astype(jnp.bfloat16), jnp.sum(p, axis=-1, keepdims=True))
    finish(*pending)


def _bias_body(g_ref, o_ref):
    period = g_ref.shape[2]
    rows = jnp.broadcast_to(g_ref[0], (QB, period))
    toeplitz = pltpu.roll(rows, 0, 1, stride=1, stride_axis=0)[:, :KWIN]
    qc = lax.broadcasted_iota(jnp.int32, (QB, KWIN), 0) // CHUNK
    kc = lax.broadcasted_iota(jnp.int32, (QB, KWIN), 1) // CHUNK
    band = (kc >= qc) & (kc <= qc + N_PREV)
    o_ref[0] = jnp.where(band, toeplitz, NEG)


def _attn_bias(rel_table):
    period = KWIN + QB
    off = N_PREV * CHUNK
    tab = rel_table.astype(jnp.float32) * LOG2E
    heads = tab.shape[0]
    far = tab[:, 2 * REL_CLIP:]
    n_far = off - REL_CLIP + 1
    near = tab[:, 1:2 * REL_CLIP][:, ::-1]
    n_low = KWIN - n_far - near.shape[1]
    g = jnp.concatenate([jnp.broadcast_to(far, (heads, n_far)), near,
                         jnp.broadcast_to(tab[:, :1], (heads, n_low)),
                         jnp.broadcast_to(far, (heads, period - KWIN))], axis=1)
    return pl.pallas_call(
        _bias_body,
        grid=(heads,),
        in_specs=[pl.BlockSpec((1, 1, period), lambda h: (h, 0, 0))],
        out_specs=pl.BlockSpec((1, QB, KWIN), lambda h: (h, 0, 0)),
        out_shape=jax.ShapeDtypeStruct((heads, QB, KWIN), jnp.float32),
        compiler_params=_cp(("arbitrary",)),
        name="attn_bias",
    )(g.reshape(heads, 1, period))


def _attention(z, bm, batch, seq):
    return pl.pallas_call(
        _attn_body,
        grid=(batch, HEADS),
        in_specs=[pl.BlockSpec((seq, DH), lambda b, h: (b, h)),
                  pl.BlockSpec((seq, DH), lambda b, h: (b, HEADS + h)),
                  pl.BlockSpec((seq, DH), lambda b, h: (b, 2 * HEADS + h)),
                  pl.BlockSpec((1, QB, KWIN), lambda b, h: (h, 0, 0))],
        out_specs=pl.BlockSpec((seq, DH), lambda b, h: (b, h)),
        out_shape=jax.ShapeDtypeStruct((batch * seq, ATT_W), jnp.bfloat16),
        compiler_params=_cp(("arbitrary", "arbitrary")),
        name="band_attn",
    )(z, z, z, bm)


def _sgu_body(u_ref, v_ref, g_ref, b_ref, w_ref, bs_ref, o_ref):
    t = u_ref.shape[0]
    u = u_ref[...].astype(jnp.float32)
    vn = _ln(v_ref[...].astype(jnp.float32), g_ref[...], b_ref[...]).astype(jnp.bfloat16)
    row = lax.broadcasted_iota(jnp.int32, (WIN, WIN), 0)
    col = lax.broadcasted_iota(jnp.int32, (WIN, WIN), 1)
    causal = row >= col
    for g in range(SGU_G):
        wm = jnp.where(causal, w_ref[g], 0.0).astype(jnp.bfloat16)
        bias = bs_ref[:, g:g + 1]
        for n in range(t // WIN):
            r0, c0 = n * WIN, g * 128
            mixed = jnp.dot(wm, vn[r0:r0 + WIN, c0:c0 + 128],
                            preferred_element_type=jnp.float32) + bias
            o_ref[r0:r0 + WIN, c0:c0 + 128] = (u[r0:r0 + WIN, c0:c0 + 128] * mixed
                                               ).astype(o_ref.dtype)


def _sgu(z, ln_g, ln_b, w_s, b_s):
    n = z.shape[0]
    t = 256
    ublk = 3 * ATT_W // SGU_W
    return pl.pallas_call(
        _sgu_body,
        grid=(n // t,),
        in_specs=[pl.BlockSpec((t, SGU_W), lambda i: (i, ublk)),
                  pl.BlockSpec((t, SGU_W), lambda i: (i, ublk + 1)),
                  pl.BlockSpec((1, SGU_W), lambda i: (0, 0)),
                  pl.BlockSpec((1, SGU_W), lambda i: (0, 0)),
                  pl.BlockSpec((SGU_G, WIN, WIN), lambda i: (0, 0, 0)),
                  pl.BlockSpec((WIN, SGU_G), lambda i: (0, 0))],
        out_specs=pl.BlockSpec((t, SGU_W), lambda i: (i, 0)),
        out_shape=jax.ShapeDtypeStruct((n, SGU_W), jnp.bfloat16),
        compiler_params=_cp(("arbitrary",)),
        name="sgu",
    )(z, z, ln_g.reshape(1, SGU_W), ln_b.reshape(1, SGU_W), w_s, b_s.T)


def _top2_of4(a, b, c, d):
    m1, n1 = jnp.maximum(a, b), jnp.minimum(a, b)
    m2, n2 = jnp.maximum(c, d), jnp.minimum(c, d)
    top = jnp.maximum(m1, m2)
    second = jnp.maximum(jnp.minimum(m1, m2), jnp.maximum(n1, n2))
    return top + second


def _route_body(l_ref, id_ref, gt_ref):
    rows = [l_ref[e:e + 1, :] for e in range(N_EXP)]
    mx = functools.reduce(jnp.maximum, rows)
    ex = [jnp.exp(r - mx) for r in rows]
    den = functools.reduce(lambda a, b: a + b, ex)
    p = [e / den for e in ex]
    score = [_top2_of4(*p[EPG * g:EPG * (g + 1)]) for g in range(N_GRP)]
    best, sel = score[0], jnp.zeros_like(score[0], dtype=jnp.int32)
    for g in range(1, N_GRP):
        better = score[g] > best
        best = jnp.where(better, score[g], best)
        sel = jnp.where(better, g, sel)
    cand = []
    for k in range(EPG):
        val = p[k]
        for g in range(1, N_GRP):
            val = jnp.where(sel == g, p[EPG * g + k], val)
        cand.append(val)
    p1, i1 = cand[0], jnp.zeros_like(sel)
    for k in range(1, EPG):
        better = cand[k] > p1
        p1 = jnp.where(better, cand[k], p1)
        i1 = jnp.where(better, k, i1)
    p2, i2 = jnp.full_like(p1, -1.0), jnp.zeros_like(sel)
    for k in range(EPG):
        better = (cand[k] > p2) & (i1 != k)
        p2 = jnp.where(better, cand[k], p2)
        i2 = jnp.where(better, k, i2)
    tot = p1 + p2
    id_ref[...] = jnp.concatenate([sel * EPG + i1, sel * EPG + i2], axis=0)
    pad = jnp.zeros((SUBLANES - 2, p1.shape[1]), jnp.float32)
    gt_ref[...] = jnp.concatenate([p1 / tot, p2 / tot, pad], axis=0)


def _route(logits):
    n = logits.shape[1]
    return pl.pallas_call(
        _route_body,
        out_shape=[jax.ShapeDtypeStruct((2, n), jnp.int32),
                   jax.ShapeDtypeStruct((SUBLANES, n), jnp.float32)],
        compiler_params=_cp(None),
        name="route",
    )(logits)


def _outproj_body(ya_ref, ys_ref, w_ref, h_ref, gin_ref, bin_ref, g_ref, b_ref,
                  rw2_ref, rb_ref, hf_ref, hp_ref, lg_ref, *, raw_input):
    mix = jnp.dot(ya_ref[...], w_ref[:ATT_W, :], preferred_element_type=jnp.float32)
    mix = mix + jnp.dot(ys_ref[...], w_ref[ATT_W:, :], preferred_element_type=jnp.float32)
    h = h_ref[...]
    if raw_input:
        h = _ln(h, gin_ref[...], bin_ref[...])
    h1 = _ln(ALPHA * h + mix, g_ref[...], b_ref[...])
    hf_ref[...] = h1
    _pack_rows(h1, hp_ref)
    hh = h1.astype(jnp.bfloat16)
    hl = (h1 - hh.astype(jnp.float32)).astype(jnp.bfloat16)
    nt = (((1,), (1,)), ((), ()))
    both = lax.dot_general(rw2_ref[...], hh, nt, preferred_element_type=jnp.float32)
    low = lax.dot_general(rw2_ref[:N_EXP, :], hl, nt, preferred_element_type=jnp.float32)
    lg_ref[...] = both[:N_EXP] + low + both[N_EXP:] + rb_ref[...]


def _outproj(ya, ys, wb, resid, gin, bin_, g, b, rw2, rb, raw_input):
    n = ya.shape[0]
    t = 512
    return pl.pallas_call(
        functools.partial(_outproj_body, raw_input=raw_input),
        grid=(n // t,),
        in_specs=[pl.BlockSpec((t, ATT_W), lambda i: (i, 0)),
                  pl.BlockSpec((t, SGU_W), lambda i: (i, 0)),
                  pl.BlockSpec((D, D), lambda i: (0, 0)),
                  pl.BlockSpec((t, D), lambda i: (i, 0)),
                  pl.BlockSpec((1, D), lambda i: (0, 0)),
                  pl.BlockSpec((1, D), lambda i: (0, 0)),
                  pl.BlockSpec((1, D), lambda i: (0, 0)),
                  pl.BlockSpec((1, D), lambda i: (0, 0)),
                  pl.BlockSpec((2 * N_EXP, D), lambda i: (0, 0)),
                  pl.BlockSpec((N_EXP, 1), lambda i: (0, 0))],
        out_specs=[pl.BlockSpec((t, D), lambda i: (i, 0)),
                   pl.BlockSpec((t * PITCH, LANES), lambda i: (i, 0)),
                   pl.BlockSpec((N_EXP, t), lambda i: (0, i))],
        out_shape=[jax.ShapeDtypeStruct((n, D), jnp.float32),
                   jax.ShapeDtypeStruct((n * PITCH, LANES), jnp.float32),
                   jax.ShapeDtypeStruct((N_EXP, n), jnp.float32)],
        compiler_params=_cp(("arbitrary",)),
        name="out_proj_ln_route",
    )(ya, ys, wb, resid, gin.reshape(1, D), bin_.reshape(1, D), g.reshape(1, D),
      b.reshape(1, D), rw2, rb)


def _weight_copies(pairs, layer, e, sem):
    copies = []
    for w_ref, stage, _ in pairs:
        rows = stage.shape[0] // W_CHUNKS
        for c in range(W_CHUNKS):
            copies.append(pltpu.make_async_copy(w_ref.at[layer, e, pl.ds(c * rows, rows)],
                                                stage.at[pl.ds(c * rows, rows)], sem))
    return copies


def _refresh_weights(pairs, layer, te_ref, tend_ref, nt, t, sem):
    e = te_ref[t]

    @pl.when((t == 0) | (e != te_ref[jnp.maximum(t - 1, 0)]))
    def _():
        for cp in _weight_copies(pairs, layer, e, sem):
            cp.wait()
        for _, stage, work in pairs:
            def cast(c, carry, stage=stage, work=work):
                rows = pl.ds(pl.multiple_of(c * CAST_ROWS, CAST_ROWS), CAST_ROWS)
                work[rows, :] = stage[rows, :].astype(jnp.bfloat16)
                return carry

            lax.fori_loop(0, stage.shape[0] // CAST_ROWS, cast, 0)
        nxt = tend_ref[e]

        @pl.when(nxt < nt)
        def _():
            e_next = te_ref[jnp.minimum(nxt, te_ref.shape[0] - 1)]
            for cp in _weight_copies(pairs, layer, e_next, sem):
                cp.start(priority=W_PRIORITY)


def _experts_body(te_ref, nt_ref, tend_ref, nvalid_ref, pos_ref,
                  hp_ref, wg_ref, wu_ref, wd_ref, y_ref,
                  xa, xb, ya, yb, wgs, wus, wds, wgb, wub, wdb, asg_ref, sems, *, layer, n):
    step = pl.program_id(0)
    nt = nt_ref[0]
    gsem = (sems.at[0], sems.at[1])
    ssem = (sems.at[2], sems.at[3])
    wsem = sems.at[4]
    pairs = ((wg_ref, wgs, wgb), (wu_ref, wus, wub), (wd_ref, wds, wdb))
    tile_rows = TM * PITCH

    def assignment(tile, r):
        valid = r < nvalid_ref[tile]
        return valid, asg_ref[jnp.where(valid, tile * TM + r, 0)]

    def gather_copy(tile, r, xbuf, sem):
        valid, a = assignment(tile, r)
        tok = jnp.where(valid, jnp.where(a >= n, a - n, a), 0)
        return pltpu.make_async_copy(hp_ref.at[pl.ds(tok * PITCH, PITCH)],
                                     xbuf.at[pl.ds(r * PITCH, PITCH)], sem)

    def scatter_copy(tile, r, ybuf, sem, dump_block):
        valid, a = assignment(jnp.maximum(tile, 0), r)
        dst = jnp.where(valid & (tile >= 0), a, 2 * n + dump_block * TM + r)
        return pltpu.make_async_copy(ybuf.at[pl.ds(r * PITCH, PITCH)],
                                     y_ref.at[pl.ds(dst * PITCH, PITCH)], sem)

    def wait_gather(xbuf, sem):
        pltpu.make_async_copy(hp_ref.at[pl.ds(0, tile_rows)], xbuf, sem).wait()

    def wait_scatter(ybuf, sem):
        pltpu.make_async_copy(ybuf, y_ref.at[pl.ds(0, tile_rows)], sem).wait()

    @pl.when(step == 0)
    def _():
        for cp in _weight_copies(pairs, layer, te_ref[0], wsem):
            cp.start(priority=W_PRIORITY)

        def invert(a, carry):
            asg_ref[pos_ref[a]] = a
            return carry

        lax.fori_loop(0, 2 * n, invert, 0, unroll=32)

        def issue(r, carry):
            gather_copy(0, r, xa, gsem[0]).start()
            return carry

        lax.fori_loop(0, TM, issue, 0)
        ya[...] = jnp.zeros_like(ya)
        yb[...] = jnp.zeros_like(yb)
        pltpu.make_async_copy(ya, y_ref.at[pl.ds(2 * n * PITCH, tile_rows)], ssem[0]).start()

    def phase(t, xcur, gcur, xnext, gnext, ycur, scur, yprev, sprev, cur_block):
        @pl.when(t < nt)
        def _():
            _refresh_weights(pairs, layer, te_ref, tend_ref, nt, t, wsem)
            wait_gather(xcur, gcur)
            tnext = jnp.minimum(t + 1, nt - 1)
            for r in range(TM):
                gather_copy(tnext, r, xnext, gnext).start()
            for r in range(TM):
                scatter_copy(t - 1, r, yprev, sprev, 1 - cur_block).start()
            x = jnp.concatenate([c.astype(jnp.bfloat16) for c in _unpack_rows(xcur, TM)],
                                axis=1)
            g = jnp.dot(x, wgb[...], preferred_element_type=jnp.float32)
            u = jnp.dot(x, wub[...], preferred_element_type=jnp.float32)
            hid = (g * jax.nn.sigmoid(g) * u).astype(jnp.bfloat16)
            y = jnp.dot(hid, wdb[...], preferred_element_type=jnp.float32)
            wait_scatter(ycur, scur)
            _pack_rows(y, ycur)

            @pl.when(t == nt - 1)
            def _():
                wait_gather(xnext, gnext)
                wait_scatter(yprev, sprev)

                def issue(r, carry):
                    scatter_copy(t, r, ycur, scur, cur_block).start()
                    return carry

                lax.fori_loop(0, TM, issue, 0)
                wait_scatter(ycur, scur)

    phase(2 * step, xa, gsem[0], xb, gsem[1], ya, ssem[0], yb, ssem[1], 0)
    phase(2 * step + 1, xb, gsem[1], xa, gsem[0], yb, ssem[1], ya, ssem[0], 1)


def _experts(tile_expert, ntiles, tile_end, nvalid, pos, hp, w_gate, w_up, w_down,
             layer, n, max_tiles):
    any_spec = pl.BlockSpec(memory_space=pl.ANY)
    slab = pltpu.VMEM((TM * PITCH, LANES), jnp.float32)
    return pl.pallas_call(
        functools.partial(_experts_body, layer=layer, n=n),
        grid_spec=pltpu.PrefetchScalarGridSpec(
            num_scalar_prefetch=5,
            grid=(max_tiles // 2,),
            in_specs=[any_spec, any_spec, any_spec, any_spec],
            out_specs=any_spec,
            scratch_shapes=[slab, slab, slab, slab,
                            pltpu.VMEM((D, F), jnp.float32),
                            pltpu.VMEM((D, F), jnp.float32),
                            pltpu.VMEM((F, D), jnp.float32),
                            pltpu.VMEM((D, F), jnp.bfloat16),
                            pltpu.VMEM((D, F), jnp.bfloat16),
                            pltpu.VMEM((F, D), jnp.bfloat16),
                            pltpu.SMEM((max_tiles * TM,), jnp.int32),
                            pltpu.SemaphoreType.DMA((5,))]),
        out_shape=jax.ShapeDtypeStruct(((2 * n + 2 * TM) * PITCH, LANES), jnp.float32),
        compiler_params=_cp(("arbitrary",), disable_bounds_checks=True),
        name="experts",
    )(tile_expert, ntiles, tile_end, nvalid, pos, hp, w_gate, w_up, w_down)


def _combine_body(y0_ref, y1_ref, h_ref, gt_ref, g_ref, b_ref, hf_ref, *maybe_hb_ref):
    t = h_ref.shape[0]
    y0 = jnp.concatenate(_unpack_rows(y0_ref, t), axis=1)
    y1 = jnp.concatenate(_unpack_rows(y1_ref, t), axis=1)
    gt = jnp.transpose(gt_ref[...])
    ffn = gt[:, 0:1] * y0 + gt[:, 1:2] * y1
    h2 = _ln(ALPHA * h_ref[...] + ffn, g_ref[...], b_ref[...])
    hf_ref[...] = h2
    for hb_ref in maybe_hb_ref:
        hb_ref[...] = h2.astype(jnp.bfloat16)


def _combine(yg, hf, gates, g, b, emit_bf16):
    n = hf.shape[0]
    t = 256
    nb = n // t
    row_spec = pl.BlockSpec((t, D), lambda i: (i, 0))
    out_specs = [row_spec] + ([row_spec] if emit_bf16 else [])
    out_shape = [jax.ShapeDtypeStruct((n, D), jnp.float32)]
    if emit_bf16:
        out_shape.append(jax.ShapeDtypeStruct((n, D), jnp.bfloat16))
    return pl.pallas_call(
        _combine_body,
        grid=(nb,),
        in_specs=[pl.BlockSpec((t * PITCH, LANES), lambda i: (i, 0)),
                  pl.BlockSpec((t * PITCH, LANES), lambda i: (nb + i, 0)),
                  row_spec,
                  pl.BlockSpec((SUBLANES, t), lambda i: (0, i)),
                  pl.BlockSpec((1, D), lambda i: (0, 0)),
                  pl.BlockSpec((1, D), lambda i: (0, 0))],
        out_specs=out_specs,
        out_shape=out_shape,
        compiler_params=_cp(("arbitrary",)),
        name="combine_ln",
    )(yg, yg, hf, gates, g.reshape(1, D), b.reshape(1, D))


def _dispatch_plan(ids, n, max_tiles):
    i32 = jnp.int32
    e_flat = ids.reshape(-1)
    experts = jnp.arange(N_EXP, dtype=i32)
    onehot = (e_flat[:, None] == experts[None, :]).astype(i32)
    csum = jnp.cumsum(onehot, axis=0)
    counts = csum[-1]
    tiles = (counts + TM - 1) // TM
    tile_end = jnp.cumsum(tiles).astype(i32)
    tile_start = tile_end - tiles
    pos = jnp.sum(onehot * (csum - 1 + (tile_start * TM)[None, :]), axis=1).astype(i32)
    tile_ids = jnp.arange(max_tiles, dtype=i32)
    tile_expert = jnp.minimum(
        jnp.sum((tile_ids[:, None] >= tile_end[None, :]).astype(i32), axis=1), N_EXP - 1)
    mine = (tile_expert[:, None] == experts[None, :]).astype(i32)
    before = (tile_ids - jnp.sum(mine * tile_start[None, :], axis=1)) * TM
    nvalid = jnp.clip(jnp.sum(mine * counts[None, :], axis=1) - before, 0, TM).astype(i32)
    return pos, nvalid, tile_expert, tile_end[-1:], tile_end


def kernel(x, ln_in_g, ln_in_b, w_in, rel_bias, sgu_ln_g, sgu_ln_b, sgu_w, sgu_b, w_out,
           ln1_g, ln1_b, router_w, router_b, w_gate, w_up, w_down, ln2_g, ln2_b):
    batch, seq, _ = x.shape
    n = batch * seq
    max_tiles = (2 * n) // TM + N_EXP
    rwt = router_w.T
    rwh = rwt.astype(jnp.bfloat16)
    rwl = (rwt - rwh.astype(jnp.float32)).astype(jnp.bfloat16)
    rw2 = jnp.concatenate([rwh, rwl], axis=0)
    rb = router_b.reshape(N_EXP, 1)

    hf = x.reshape(n, D)
    hb = _ln0(hf, ln_in_g, ln_in_b)
    for l in range(DEPTH):
        z = _inproj(hb, w_in, l)
        ya = _attention(z, _attn_bias(rel_bias[l]), batch, seq)
        ys = _sgu(z, sgu_ln_g[l], sgu_ln_b[l], sgu_w[l], sgu_b[l])
        hf, hp, logits = _outproj(ya, ys, w_out[l].astype(jnp.bfloat16), hf,
                                  ln_in_g, ln_in_b, ln1_g[l], ln1_b[l], rw2, rb,
                                  raw_input=(l == 0))
        ids, gates = _route(logits)
        pos, nvalid, tile_expert, ntiles, tile_end = _dispatch_plan(ids, n, max_tiles)
        yslots = _experts(tile_expert, ntiles, tile_end, nvalid, pos, hp,
                          w_gate, w_up, w_down, l, n, max_tiles)
        outs = _combine(yslots, hf, gates, ln2_g[l], ln2_b[l], emit_bf16=(l + 1 < DEPTH))
        hf = outs[0]
        hb = outs[1] if l + 1 < DEPTH else None
    return hf.reshape(batch, seq, D)
```

```python
import functools
import math

import jax
import jax.numpy as jnp
import numpy as np
from jax import lax
from jax.experimental import pallas as pl
from jax.experimental.pallas import tpu as pltpu

D = 2048
HEADS = 8
DH = 128
ATT_W = HEADS * DH
SGU_G = 8
SGU_W = SGU_G * 128
WIN = 128
IN_W = 3 * ATT_W + 2 * SGU_W
CHUNK = 64
N_PREV = 8
REL_CLIP = 128
N_EXP = 16
N_GRP = 4
EPG = 4
F = 1024
DEPTH = 2
ALPHA = (2.0 * DEPTH) ** 0.25
EPS = 1e-5
NEG = -1e30
LOG2E = math.log2(math.e)

LANES = 128
SUBLANES = 8
VMEM_LIMIT = 56 * 1024 * 1024

QB = 256
KWIN = QB + N_PREV * CHUNK
TM = 256
WPT = D // LANES
W_CHUNKS = 4
CAST_ROWS = 128
PITCH = WPT + 1
W_PRIORITY = 1


def _cp(sem, vmem=VMEM_LIMIT, **kw):
    return pltpu.CompilerParams(dimension_semantics=sem, vmem_limit_bytes=vmem, **kw)


def _ln(x, g, b):
    mu = jnp.mean(x, axis=-1, keepdims=True)
    xc = x - mu
    var = jnp.mean(xc * xc, axis=-1, keepdims=True)
    return xc * lax.rsqrt(var + EPS) * g + b


def _pack_rows(y, o_ref, row0=0):
    t = y.shape[0]
    base = row0 * PITCH
    for s in range(WPT):
        o_ref[pl.ds(base + s, t, stride=PITCH), :] = y[:, LANES * s:LANES * (s + 1)]
    for s in range(WPT, PITCH):
        o_ref[pl.ds(base + s, t, stride=PITCH), :] = jnp.zeros((t, LANES), jnp.float32)


def _unpack_rows(p_ref, t):
    return [p_ref[pl.ds(s, t, stride=PITCH), :] for s in range(WPT)]


def _ln0_body(x_ref, g_ref, b_ref, hb_ref):
    hb_ref[...] = _ln(x_ref[...], g_ref[...], b_ref[...]).astype(jnp.bfloat16)


def _ln0(x, g, b):
    n = x.shape[0]
    t = 512
    return pl.pallas_call(
        _ln0_body,
        grid=(n // t,),
        in_specs=[pl.BlockSpec((t, D), lambda i: (i, 0)),
                  pl.BlockSpec((1, D), lambda i: (0, 0)),
                  pl.BlockSpec((1, D), lambda i: (0, 0))],
        out_specs=pl.BlockSpec((t, D), lambda i: (i, 0)),
        out_shape=jax.ShapeDtypeStruct((n, D), jnp.bfloat16),
        compiler_params=_cp(("arbitrary",)),
        name="ln_in",
    )(x, g.reshape(1, D), b.reshape(1, D))


def _gelu(x):
    return 0.5 * x * (1.0 + lax.erf(x * (1.0 / math.sqrt(2.0))))


def _inproj_body(x_ref, w_ref, o_ref, wb_ref, *, gelu_from):
    j = pl.program_id(0)

    @pl.when(pl.program_id(1) == 0)
    def _():
        wb_ref[...] = w_ref[0].astype(jnp.bfloat16)

    @pl.when(j < gelu_from)
    def _():
        o_ref[...] = jnp.dot(x_ref[...], wb_ref[...],
                             preferred_element_type=jnp.float32).astype(o_ref.dtype)

    @pl.when(j >= gelu_from)
    def _():
        o_ref[...] = _gelu(jnp.dot(x_ref[...], wb_ref[...],
                                   preferred_element_type=jnp.float32)).astype(o_ref.dtype)


def _inproj(hb, w, layer):
    n = hb.shape[0]
    tm, tn = 1024, 1024
    return pl.pallas_call(
        functools.partial(_inproj_body, gelu_from=3 * ATT_W // tn),
        grid=(IN_W // tn, n // tm),
        in_specs=[pl.BlockSpec((tm, D), lambda j, i: (i, 0)),
                  pl.BlockSpec((1, D, tn), lambda j, i: (layer, 0, j))],
        out_specs=pl.BlockSpec((tm, tn), lambda j, i: (i, j)),
        out_shape=jax.ShapeDtypeStruct((n, IN_W), jnp.bfloat16),
        scratch_shapes=[pltpu.VMEM((D, tn), jnp.bfloat16)],
        compiler_params=_cp(("arbitrary", "arbitrary")),
        name="in_proj",
    )(hb, w)


def _attn_body(q_ref, k_ref, v_ref, bm_ref, o_ref):
    seq = q_ref.shape[0]
    scale = LOG2E / math.sqrt(DH)
    def window(j):
        q0 = j * QB
        k0 = max(0, q0 - N_PREV * CHUNK)
        return q0, k0, q0 + QB - k0

    def scores(j):
        q0, k0, nk = window(j)
        return lax.dot_general(q_ref[q0:q0 + QB, :], k_ref[k0:k0 + nk, :],
                               (((1,), (1,)), ((), ())), preferred_element_type=jnp.float32)

    def finish(j, p, l):
        q0, k0, nk = window(j)
        o = jnp.dot(p, v_ref[k0:k0 + nk, :], preferred_element_type=jnp.float32)
        o_ref[q0:q0 + QB, :] = (o / l).astype(o_ref.dtype)

    blocks = seq // QB
    s_next = scores(0)
    pending = None
    for j in range(blocks):
        _, _, nk = window(j)
        s = s_next
        if j + 1 < blocks:
            s_next = scores(j + 1)
        if pending is not None:
            finish(*pending)
        s = s * scale + bm_ref[0, :, KWIN - nk:]
        m = jnp.max(s, axis=-1, keepdims=True)
        p = jnp.exp2(s - m)
        pending = (j, p.astype(jnp.bfloat16), jnp.sum(p, axis=-1, keepdims=True))
    finish(*pending)


def _bias_body(g_ref, o_ref):
    period = g_ref.shape[2]
    rows = jnp.broadcast_to(g_ref[0], (QB, period))
    toeplitz = pltpu.roll(rows, 0, 1, stride=1, stride_axis=0)[:, :KWIN]
    qc = lax.broadcasted_iota(jnp.int32, (QB, KWIN), 0) // CHUNK
    kc = lax.broadcasted_iota(jnp.int32, (QB, KWIN), 1) // CHUNK
    band = (kc >= qc) & (kc <= qc + N_PREV)
    o_ref[0] = jnp.where(band, toeplitz, NEG)


def _attn_bias(rel_table):
    period = KWIN + QB
    off = N_PREV * CHUNK
    tab = rel_table.astype(jnp.float32) * LOG2E
    heads = tab.shape[0]
    far = tab[:, 2 * REL_CLIP:]
    n_far = off - REL_CLIP + 1
    near = tab[:, 1:2 * REL_CLIP][:, ::-1]
    n_low = KWIN - n_far - near.shape[1]
    g = jnp.concatenate([jnp.broadcast_to(far, (heads, n_far)), near,
                         jnp.broadcast_to(tab[:, :1], (heads, n_low)),
                         jnp.broadcast_to(far, (heads, period - KWIN))], axis=1)
    return pl.pallas_call(
        _bias_body,
        grid=(heads,),
        in_specs=[pl.BlockSpec((1, 1, period), lambda h: (h, 0, 0))],
        out_specs=pl.BlockSpec((1, QB, KWIN), lambda h: (h, 0, 0)),
        out_shape=jax.ShapeDtypeStruct((heads, QB, KWIN), jnp.float32),
        compiler_params=_cp(("arbitrary",)),
        name="attn_bias",
    )(g.reshape(heads, 1, period))


def _attention(z, bm, batch, seq):
    return pl.pallas_call(
        _attn_body,
        grid=(batch, HEADS),
        in_specs=[pl.BlockSpec((seq, DH), lambda b, h: (b, h)),
                  pl.BlockSpec((seq, DH), lambda b, h: (b, HEADS + h)),
                  pl.BlockSpec((seq, DH), lambda b, h: (b, 2 * HEADS + h)),
                  pl.BlockSpec((1, QB, KWIN), lambda b, h: (h, 0, 0))],
        out_specs=pl.BlockSpec((seq, DH), lambda b, h: (b, h)),
        out_shape=jax.ShapeDtypeStruct((batch * seq, ATT_W), jnp.bfloat16),
        compiler_params=_cp(("arbitrary", "arbitrary")),
        name="band_attn",
    )(z, z, z, bm)


def _sgu_body(u_ref, v_ref, g_ref, b_ref, w_ref, bs_ref, o_ref):
    t = u_ref.shape[0]
    u = u_ref[...].astype(jnp.float32)
    vn = _ln(v_ref[...].astype(jnp.float32), g_ref[...], b_ref[...]).astype(jnp.bfloat16)
    row = lax.broadcasted_iota(jnp.int32, (WIN, WIN), 0)
    col = lax.broadcasted_iota(jnp.int32, (WIN, WIN), 1)
    causal = row >= col
    for g in range(SGU_G):
        wm = jnp.where(causal, w_ref[g], 0.0).astype(jnp.bfloat16)
        bias = bs_ref[:, g:g + 1]
        for n in range(t // WIN):
            r0, c0 = n * WIN, g * 128
            mixed = jnp.dot(wm, vn[r0:r0 + WIN, c0:c0 + 128],
                            preferred_element_type=jnp.float32) + bias
            o_ref[r0:r0 + WIN, c0:c0 + 128] = (u[r0:r0 + WIN, c0:c0 + 128] * mixed
                                               ).astype(o_ref.dtype)


def _sgu(z, ln_g, ln_b, w_s, b_s):
    n = z.shape[0]
    t = 512
    ublk = 3 * ATT_W // SGU_W
    return pl.pallas_call(
        _sgu_body,
        grid=(n // t,),
        in_specs=[pl.BlockSpec((t, SGU_W), lambda i: (i, ublk)),
                  pl.BlockSpec((t, SGU_W), lambda i: (i, ublk + 1)),
                  pl.BlockSpec((1, SGU_W), lambda i: (0, 0)),
                  pl.BlockSpec((1, SGU_W), lambda i: (0, 0)),
                  pl.BlockSpec((SGU_G, WIN, WIN), lambda i: (0, 0, 0)),
                  pl.BlockSpec((WIN, SGU_G), lambda i: (0, 0))],
        out_specs=pl.BlockSpec((t, SGU_W), lambda i: (i, 0)),
        out_shape=jax.ShapeDtypeStruct((n, SGU_W), jnp.bfloat16),
        compiler_params=_cp(("arbitrary",)),
        name="sgu",
    )(z, z, ln_g.reshape(1, SGU_W), ln_b.reshape(1, SGU_W), w_s, b_s.T)


def _top2_of4(a, b, c, d):
    m1, n1 = jnp.maximum(a, b), jnp.minimum(a, b)
    m2, n2 = jnp.maximum(c, d), jnp.minimum(c, d)
    top = jnp.maximum(m1, m2)
    second = jnp.maximum(jnp.minimum(m1, m2), jnp.maximum(n1, n2))
    return top + second


def _route_body(l_ref, id_ref, gt_ref):
    rows = [l_ref[e:e + 1, :] for e in range(N_EXP)]
    mx = functools.reduce(jnp.maximum, rows)
    ex = [jnp.exp(r - mx) for r in rows]
    den = functools.reduce(lambda a, b: a + b, ex)
    p = [e / den for e in ex]
    score = [_top2_of4(*p[EPG * g:EPG * (g + 1)]) for g in range(N_GRP)]
    best, sel = score[0], jnp.zeros_like(score[0], dtype=jnp.int32)
    for g in range(1, N_GRP):
        better = score[g] > best
        best = jnp.where(better, score[g], best)
        sel = jnp.where(better, g, sel)
    cand = []
    for k in range(EPG):
        val = p[k]
        for g in range(1, N_GRP):
            val = jnp.where(sel == g, p[EPG * g + k], val)
        cand.append(val)
    p1, i1 = cand[0], jnp.zeros_like(sel)
    for k in range(1, EPG):
        better = cand[k] > p1
        p1 = jnp.where(better, cand[k], p1)
        i1 = jnp.where(better, k, i1)
    p2, i2 = jnp.full_like(p1, -1.0), jnp.zeros_like(sel)
    for k in range(EPG):
        better = (cand[k] > p2) & (i1 != k)
        p2 = jnp.where(better, cand[k], p2)
        i2 = jnp.where(better, k, i2)
    tot = p1 + p2
    id_ref[...] = jnp.concatenate([sel * EPG + i1, sel * EPG + i2], axis=0)
    pad = jnp.zeros((SUBLANES - 2, p1.shape[1]), jnp.float32)
    gt_ref[...] = jnp.concatenate([p1 / tot, p2 / tot, pad], axis=0)


def _route(logits):
    n = logits.shape[1]
    return pl.pallas_call(
        _route_body,
        out_shape=[jax.ShapeDtypeStruct((2, n), jnp.int32),
                   jax.ShapeDtypeStruct((SUBLANES, n), jnp.float32)],
        compiler_params=_cp(None),
        name="route",
    )(logits)


def _outproj_body(ya_ref, ys_ref, w_ref, h_ref, gin_ref, bin_ref, g_ref, b_ref,
                  rw2_ref, rb_ref, hf_ref, hp_ref, lg_ref, *, raw_input):
    mix = jnp.dot(ya_ref[...], w_ref[:ATT_W, :], preferred_element_type=jnp.float32)
    mix = mix + jnp.dot(ys_ref[...], w_ref[ATT_W:, :], preferred_element_type=jnp.float32)
    h = h_ref[...]
    if raw_input:
        h = _ln(h, gin_ref[...], bin_ref[...])
    h1 = _ln(ALPHA * h + mix, g_ref[...], b_ref[...])
    hf_ref[...] = h1
    _pack_rows(h1, hp_ref)
    hh = h1.astype(jnp.bfloat16)
    hl = (h1 - hh.astype(jnp.float32)).astype(jnp.bfloat16)
    nt = (((1,), (1,)), ((), ()))
    both = lax.dot_general(rw2_ref[...], hh, nt, preferred_element_type=jnp.float32)
    low = lax.dot_general(rw2_ref[:N_EXP, :], hl, nt, preferred_element_type=jnp.float32)
    lg_ref[...] = both[:N_EXP] + low + both[N_EXP:] + rb_ref[...]


def _outproj(ya, ys, wb, resid, gin, bin_, g, b, rw2, rb, raw_input):
    n = ya.shape[0]
    t = 512
    return pl.pallas_call(
        functools.partial(_outproj_body, raw_input=raw_input),
        grid=(n // t,),
        in_specs=[pl.BlockSpec((t, ATT_W), lambda i: (i, 0)),
                  pl.BlockSpec((t, SGU_W), lambda i: (i, 0)),
                  pl.BlockSpec((D, D), lambda i: (0, 0)),
                  pl.BlockSpec((t, D), lambda i: (i, 0)),
                  pl.BlockSpec((1, D), lambda i: (0, 0)),
                  pl.BlockSpec((1, D), lambda i: (0, 0)),
                  pl.BlockSpec((1, D), lambda i: (0, 0)),
                  pl.BlockSpec((1, D), lambda i: (0, 0)),
                  pl.BlockSpec((2 * N_EXP, D), lambda i: (0, 0)),
                  pl.BlockSpec((N_EXP, 1), lambda i: (0, 0))],
        out_specs=[pl.BlockSpec((t, D), lambda i: (i, 0)),
                   pl.BlockSpec((t * PITCH, LANES), lambda i: (i, 0)),
                   pl.BlockSpec((N_EXP, t), lambda i: (0, i))],
        out_shape=[jax.ShapeDtypeStruct((n, D), jnp.float32),
                   jax.ShapeDtypeStruct((n * PITCH, LANES), jnp.float32),
                   jax.ShapeDtypeStruct((N_EXP, n), jnp.float32)],
        compiler_params=_cp(("arbitrary",)),
        name="out_proj_ln_route",
    )(ya, ys, wb, resid, gin.reshape(1, D), bin_.reshape(1, D), g.reshape(1, D),
      b.reshape(1, D), rw2, rb)


def _weight_copies(pairs, layer, e, sem):
    copies = []
    for w_ref, stage, _ in pairs:
        rows = stage.shape[0] // W_CHUNKS
        for c in range(W_CHUNKS):
            copies.append(pltpu.make_async_copy(w_ref.at[layer, e, pl.ds(c * rows, rows)],
                                                stage.at[pl.ds(c * rows, rows)], sem))
    return copies


def _refresh_weights(pairs, layer, te_ref, tend_ref, nt, t, sem):
    e = te_ref[t]

    @pl.when((t == 0) | (e != te_ref[jnp.maximum(t - 1, 0)]))
    def _():
        for cp in _weight_copies(pairs, layer, e, sem):
            cp.wait()
        for _, stage, work in pairs:
            def cast(c, carry, stage=stage, work=work):
                rows = pl.ds(pl.multiple_of(c * CAST_ROWS, CAST_ROWS), CAST_ROWS)
                work[rows, :] = stage[rows, :].astype(jnp.bfloat16)
                return carry

            lax.fori_loop(0, stage.shape[0] // CAST_ROWS, cast, 0)
        nxt = tend_ref[e]

        @pl.when(nxt < nt)
        def _():
            e_next = te_ref[jnp.minimum(nxt, te_ref.shape[0] - 1)]
            for cp in _weight_copies(pairs, layer, e_next, sem):
                cp.start(priority=W_PRIORITY)


def _experts_body(te_ref, nt_ref, tend_ref, nvalid_ref, pos_ref,
                  hp_ref, wg_ref, wu_ref, wd_ref, y_ref,
                  xa, xb, ya, yb, wgs, wus, wds, wgb, wub, wdb, asg_ref, sems, *, layer, n):
    step = pl.program_id(0)
    nt = nt_ref[0]
    gsem = (sems.at[0], sems.at[1])
    ssem = (sems.at[2], sems.at[3])
    wsem = sems.at[4]
    pairs = ((wg_ref, wgs, wgb), (wu_ref, wus, wub), (wd_ref, wds, wdb))
    tile_rows = TM * PITCH

    def assignment(tile, r):
        valid = r < nvalid_ref[tile]
        return valid, asg_ref[jnp.where(valid, tile * TM + r, 0)]

    def gather_copy(tile, r, xbuf, sem):
        valid, a = assignment(tile, r)
        tok = jnp.where(valid, jnp.where(a >= n, a - n, a), 0)
        return pltpu.make_async_copy(hp_ref.at[pl.ds(tok * PITCH, PITCH)],
                                     xbuf.at[pl.ds(r * PITCH, PITCH)], sem)

    def scatter_copy(tile, r, ybuf, sem, dump_block):
        valid, a = assignment(jnp.maximum(tile, 0), r)
        dst = jnp.where(valid & (tile >= 0), a, 2 * n + dump_block * TM + r)
        return pltpu.make_async_copy(ybuf.at[pl.ds(r * PITCH, PITCH)],
                                     y_ref.at[pl.ds(dst * PITCH, PITCH)], sem)

    def wait_gather(xbuf, sem):
        pltpu.make_async_copy(hp_ref.at[pl.ds(0, tile_rows)], xbuf, sem).wait()

    def wait_scatter(ybuf, sem):
        pltpu.make_async_copy(ybuf, y_ref.at[pl.ds(0, tile_rows)], sem).wait()

    @pl.when(step == 0)
    def _():
        for cp in _weight_copies(pairs, layer, te_ref[0], wsem):
            cp.start(priority=W_PRIORITY)

        def invert(a, carry):
            asg_ref[pos_ref[a]] = a
            return carry

        lax.fori_loop(0, 2 * n, invert, 0, unroll=32)

        def issue(r, carry):
            gather_copy(0, r, xa, gsem[0]).start()
            return carry

        lax.fori_loop(0, TM, issue, 0)
        ya[...] = jnp.zeros_like(ya)
        yb[...] = jnp.zeros_like(yb)
        pltpu.make_async_copy(ya, y_ref.at[pl.ds(2 * n * PITCH, tile_rows)], ssem[0]).start()

    def phase(t, xcur, gcur, xnext, gnext, ycur, scur, yprev, sprev, cur_block):
        @pl.when(t < nt)
        def _():
            _refresh_weights(pairs, layer, te_ref, tend_ref, nt, t, wsem)
            wait_gather(xcur, gcur)
            tnext = jnp.minimum(t + 1, nt - 1)
            for r in range(TM):
                gather_copy(tnext, r, xnext, gnext).start()
            for r in range(TM):
                scatter_copy(t - 1, r, yprev, sprev, 1 - cur_block).start()
            x = jnp.concatenate([c.astype(jnp.bfloat16) for c in _unpack_rows(xcur, TM)],
                                axis=1)
            g = jnp.dot(x, wgb[...], preferred_element_type=jnp.float32)
            u = jnp.dot(x, wub[...], preferred_element_type=jnp.float32)
            hid = (g * jax.nn.sigmoid(g) * u).astype(jnp.bfloat16)
            y = jnp.dot(hid, wdb[...], preferred_element_type=jnp.float32)
            wait_scatter(ycur, scur)
            _pack_rows(y, ycur)

            @pl.when(t == nt - 1)
            def _():
                wait_gather(xnext, gnext)
                wait_scatter(yprev, sprev)

                def issue(r, carry):
                    scatter_copy(t, r, ycur, scur, cur_block).start()
                    return carry

                lax.fori_loop(0, TM, issue, 0)
                wait_scatter(ycur, scur)

    phase(2 * step, xa, gsem[0], xb, gsem[1], ya, ssem[0], yb, ssem[1], 0)
    phase(2 * step + 1, xb, gsem[1], xa, gsem[0], yb, ssem[1], ya, ssem[0], 1)


def _experts(tile_expert, ntiles, tile_end, nvalid, pos, hp, w_gate, w_up, w_down,
             layer, n, max_tiles):
    any_spec = pl.BlockSpec(memory_space=pl.ANY)
    slab = pltpu.VMEM((TM * PITCH, LANES), jnp.float32)
    return pl.pallas_call(
        functools.partial(_experts_body, layer=layer, n=n),
        grid_spec=pltpu.PrefetchScalarGridSpec(
            num_scalar_prefetch=5,
            grid=(max_tiles // 2,),
            in_specs=[any_spec, any_spec, any_spec, any_spec],
            out_specs=any_spec,
            scratch_shapes=[slab, slab, slab, slab,
                            pltpu.VMEM((D, F), jnp.float32),
                            pltpu.VMEM((D, F), jnp.float32),
                            pltpu.VMEM((F, D), jnp.float32),
                            pltpu.VMEM((D, F), jnp.bfloat16),
                            pltpu.VMEM((D, F), jnp.bfloat16),
                            pltpu.VMEM((F, D), jnp.bfloat16),
                            pltpu.SMEM((max_tiles * TM,), jnp.int32),
                            pltpu.SemaphoreType.DMA((5,))]),
        out_shape=jax.ShapeDtypeStruct(((2 * n + 2 * TM) * PITCH, LANES), jnp.float32),
        compiler_params=_cp(("arbitrary",), disable_bounds_checks=True),
        name="experts",
    )(tile_expert, ntiles, tile_end, nvalid, pos, hp, w_gate, w_up, w_down)


def _combine_body(y0_ref, y1_ref, h_ref, gt_ref, g_ref, b_ref, hf_ref, *maybe_hb_ref):
    t = h_ref.shape[0]
    y0 = jnp.concatenate(_unpack_rows(y0_ref, t), axis=1)
    y1 = jnp.concatenate(_unpack_rows(y1_ref, t), axis=1)
    gt = jnp.transpose(gt_ref[...])
    ffn = gt[:, 0:1] * y0 + gt[:, 1:2] * y1
    h2 = _ln(ALPHA * h_ref[...] + ffn, g_ref[...], b_ref[...])
    hf_ref[...] = h2
    for hb_ref in maybe_hb_ref:
        hb_ref[...] = h2.astype(jnp.bfloat16)


def _combine(yg, hf, gates, g, b, emit_bf16):
    n = hf.shape[0]
    t = 512
    nb = n // t
    row_spec = pl.BlockSpec((t, D), lambda i: (i, 0))
    out_specs = [row_spec] + ([row_spec] if emit_bf16 else [])
    out_shape = [jax.ShapeDtypeStruct((n, D), jnp.float32)]
    if emit_bf16:
        out_shape.append(jax.ShapeDtypeStruct((n, D), jnp.bfloat16))
    return pl.pallas_call(
        _combine_body,
        grid=(nb,),
        in_specs=[pl.BlockSpec((t * PITCH, LANES), lambda i: (i, 0)),
                  pl.BlockSpec((t * PITCH, LANES), lambda i: (nb + i, 0)),
                  row_spec,
                  pl.BlockSpec((SUBLANES, t), lambda i: (0, i)),
                  pl.BlockSpec((1, D), lambda i: (0, 0)),
                  pl.BlockSpec((1, D), lambda i: (0, 0))],
        out_specs=out_specs,
        out_shape=out_shape,
        compiler_params=_cp(("arbitrary",)),
        name="combine_ln",
    )(yg, yg, hf, gates, g.reshape(1, D), b.reshape(1, D))


def _dispatch_plan(ids, n, max_tiles):
    i32 = jnp.int32
    e_flat = ids.reshape(-1)
    experts = jnp.arange(N_EXP, dtype=i32)
    onehot = (e_flat[:, None] == experts[None, :]).astype(i32)
    csum = jnp.cumsum(onehot, axis=0)
    counts = csum[-1]
    tiles = (counts + TM - 1) // TM
    tile_end = jnp.cumsum(tiles).astype(i32)
    tile_start = tile_end - tiles
    pos = jnp.sum(onehot * (csum - 1 + (tile_start * TM)[None, :]), axis=1).astype(i32)
    tile_ids = jnp.arange(max_tiles, dtype=i32)
    tile_expert = jnp.minimum(
        jnp.sum((tile_ids[:, None] >= tile_end[None, :]).astype(i32), axis=1), N_EXP - 1)
    mine = (tile_expert[:, None] == experts[None, :]).astype(i32)
    before = (tile_ids - jnp.sum(mine * tile_start[None, :], axis=1)) * TM
    nvalid = jnp.clip(jnp.sum(mine * counts[None, :], axis=1) - before, 0, TM).astype(i32)
    return pos, nvalid, tile_expert, tile_end[-1:], tile_end


def kernel(x, ln_in_g, ln_in_b, w_in, rel_bias, sgu_ln_g, sgu_ln_b, sgu_w, sgu_b, w_out,
           ln1_g, ln1_b, router_w, router_b, w_gate, w_up, w_down, ln2_g, ln2_b):
    batch, seq, _ = x.shape
    n = batch * seq
    max_tiles = (2 * n) // TM + N_EXP
    rwt = router_w.T
    rwh = rwt.astype(jnp.bfloat16)
    rwl = (rwt - rwh.astype(jnp.float32)).astype(jnp.bfloat16)
    rw2 = jnp.concatenate([rwh, rwl], axis=0)
    rb = router_b.reshape(N_EXP, 1)

    hf = x.reshape(n, D)
    hb = _ln0(hf, ln_in_g, ln_in_b)
    for l in range(DEPTH):
        z = _inproj(hb, w_in, l)
        ya = _attention(z, _attn_bias(rel_bias[l]), batch, seq)
        ys = _sgu(z, sgu_ln_g[l], sgu_ln_b[l], sgu_w[l], sgu_b[l])
        hf, hp, logits = _outproj(ya, ys, w_out[l].astype(jnp.bfloat16), hf,
                                  ln_in_g, ln_in_b, ln1_g[l], ln1_b[l], rw2, rb,
                                  raw_input=(l == 0))
        ids, gates = _route(logits)
        pos, nvalid, tile_expert, ntiles, tile_end = _dispatch_plan(ids, n, max_tiles)
        yslots = _experts(tile_expert, ntiles, tile_end, nvalid, pos, hp,
                          w_gate, w_up, w_down, l, n, max_tiles)
        outs = _combine(yslots, hf, gates, ln2_g[l], ln2_b[l], emit_bf16=(l + 1 < DEPTH))
        hf = outs[0]
        hb = outs[1] if l + 1 < DEPTH else None
    return hf.reshape(batch, seq, D)
```

```python
import functools
import math

import jax
import jax.numpy as jnp
import numpy as np
from jax import lax
from jax.experimental import pallas as pl
from jax.experimental.pallas import tpu as pltpu

D = 2048
HEADS = 8
DH = 128
ATT_W = HEADS * DH
SGU_G = 8
SGU_W = SGU_G * 128
WIN = 128
IN_W = 3 * ATT_W + 2 * SGU_W
CHUNK = 64
N_PREV = 8
REL_CLIP = 128
N_EXP = 16
N_GRP = 4
EPG = 4
F = 1024
DEPTH = 2
ALPHA = (2.0 * DEPTH) ** 0.25
EPS = 1e-5
NEG = -1e30
LOG2E = math.log2(math.e)

LANES = 128
SUBLANES = 8
VMEM_LIMIT = 56 * 1024 * 1024

QB = 256
KWIN = QB + N_PREV * CHUNK
TM = 384
WPT = D // LANES
W_CHUNKS = 4
CAST_ROWS = 128
PITCH = WPT + 1
W_PRIORITY = 1


def _cp(sem, vmem=VMEM_LIMIT, **kw):
    return pltpu.CompilerParams(dimension_semantics=sem, vmem_limit_bytes=vmem, **kw)


def _ln(x, g, b):
    mu = jnp.mean(x, axis=-1, keepdims=True)
    xc = x - mu
    var = jnp.mean(xc * xc, axis=-1, keepdims=True)
    return xc * lax.rsqrt(var + EPS) * g + b


def _pack_rows(y, o_ref, row0=0):
    t = y.shape[0]
    base = row0 * PITCH
    for s in range(WPT):
        o_ref[pl.ds(base + s, t, stride=PITCH), :] = y[:, LANES * s:LANES * (s + 1)]
    for s in range(WPT, PITCH):
        o_ref[pl.ds(base + s, t, stride=PITCH), :] = jnp.zeros((t, LANES), jnp.float32)


def _unpack_rows(p_ref, t):
    return [p_ref[pl.ds(s, t, stride=PITCH), :] for s in range(WPT)]


def _ln0_body(x_ref, g_ref, b_ref, hb_ref):
    hb_ref[...] = _ln(x_ref[...], g_ref[...], b_ref[...]).astype(jnp.bfloat16)


def _ln0(x, g, b):
    n = x.shape[0]
    t = 512
    return pl.pallas_call(
        _ln0_body,
        grid=(n // t,),
        in_specs=[pl.BlockSpec((t, D), lambda i: (i, 0)),
                  pl.BlockSpec((1, D), lambda i: (0, 0)),
                  pl.BlockSpec((1, D), lambda i: (0, 0))],
        out_specs=pl.BlockSpec((t, D), lambda i: (i, 0)),
        out_shape=jax.ShapeDtypeStruct((n, D), jnp.bfloat16),
        compiler_params=_cp(("arbitrary",)),
        name="ln_in",
    )(x, g.reshape(1, D), b.reshape(1, D))


def _gelu(x):
    return 0.5 * x * (1.0 + lax.erf(x * (1.0 / math.sqrt(2.0))))


def _inproj_body(x_ref, w_ref, o_ref, wb_ref, *, gelu_from):
    j = pl.program_id(0)

    @pl.when(pl.program_id(1) == 0)
    def _():
        wb_ref[...] = w_ref[0].astype(jnp.bfloat16)

    @pl.when(j < gelu_from)
    def _():
        o_ref[...] = jnp.dot(x_ref[...], wb_ref[...],
                             preferred_element_type=jnp.float32).astype(o_ref.dtype)

    @pl.when(j >= gelu_from)
    def _():
        o_ref[...] = _gelu(jnp.dot(x_ref[...], wb_ref[...],
                                   preferred_element_type=jnp.float32)).astype(o_ref.dtype)


def _inproj(hb, w, layer):
    n = hb.shape[0]
    tm, tn = 1024, 1024
    return pl.pallas_call(
        functools.partial(_inproj_body, gelu_from=3 * ATT_W // tn),
        grid=(IN_W // tn, n // tm),
        in_specs=[pl.BlockSpec((tm, D), lambda j, i: (i, 0)),
                  pl.BlockSpec((1, D, tn), lambda j, i: (layer, 0, j))],
        out_specs=pl.BlockSpec((tm, tn), lambda j, i: (i, j)),
        out_shape=jax.ShapeDtypeStruct((n, IN_W), jnp.bfloat16),
        scratch_shapes=[pltpu.VMEM((D, tn), jnp.bfloat16)],
        compiler_params=_cp(("arbitrary", "arbitrary")),
        name="in_proj",
    )(hb, w)


def _attn_body(q_ref, k_ref, v_ref, bm_ref, o_ref):
    seq = q_ref.shape[0]
    scale = LOG2E / math.sqrt(DH)
    def window(j):
        q0 = j * QB
        k0 = max(0, q0 - N_PREV * CHUNK)
        return q0, k0, q0 + QB - k0

    def scores(j):
        q0, k0, nk = window(j)
        return lax.dot_general(q_ref[q0:q0 + QB, :], k_ref[k0:k0 + nk, :],
                               (((1,), (1,)), ((), ())), preferred_element_type=jnp.float32)

    def finish(j, p, l):
        q0, k0, nk = window(j)
        o = jnp.dot(p, v_ref[k0:k0 + nk, :], preferred_element_type=jnp.float32)
        o_ref[q0:q0 + QB, :] = (o / l).astype(o_ref.dtype)

    blocks = seq // QB
    s_next = scores(0)
    pending = None
    for j in range(blocks):
        _, _, nk = window(j)
        s = s_next
        if j + 1 < blocks:
            s_next = scores(j + 1)
        if pending is not None:
            finish(*pending)
        s = s * scale + bm_ref[0, :, KWIN - nk:]
        m = jnp.max(s, axis=-1, keepdims=True)
        p = jnp.exp2(s - m)
        pending = (j, p.astype(jnp.bfloat16), jnp.sum(p, axis=-1, keepdims=True))
    finish(*pending)


def _bias_body(g_ref, o_ref):
    period = g_ref.shape[2]
    rows = jnp.broadcast_to(g_ref[0], (QB, period))
    toeplitz = pltpu.roll(rows, 0, 1, stride=1, stride_axis=0)[:, :KWIN]
    qc = lax.broadcasted_iota(jnp.int32, (QB, KWIN), 0) // CHUNK
    kc = lax.broadcasted_iota(jnp.int32, (QB, KWIN), 1) // CHUNK
    band = (kc >= qc) & (kc <= qc + N_PREV)
    o_ref[0] = jnp.where(band, toeplitz, NEG)


def _attn_bias(rel_table):
    period = KWIN + QB
    off = N_PREV * CHUNK
    tab = rel_table.astype(jnp.float32) * LOG2E
    heads = tab.shape[0]
    far = tab[:, 2 * REL_CLIP:]
    n_far = off - REL_CLIP + 1
    near = tab[:, 1:2 * REL_CLIP][:, ::-1]
    n_low = KWIN - n_far - near.shape[1]
    g = jnp.concatenate([jnp.broadcast_to(far, (heads, n_far)), near,
                         jnp.broadcast_to(tab[:, :1], (heads, n_low)),
                         jnp.broadcast_to(far, (heads, period - KWIN))], axis=1)
    return pl.pallas_call(
        _bias_body,
        grid=(heads,),
        in_specs=[pl.BlockSpec((1, 1, period), lambda h: (h, 0, 0))],
        out_specs=pl.BlockSpec((1, QB, KWIN), lambda h: (h, 0, 0)),
        out_shape=jax.ShapeDtypeStruct((heads, QB, KWIN), jnp.float32),
        compiler_params=_cp(("arbitrary",)),
        name="attn_bias",
    )(g.reshape(heads, 1, period))


def _attention(z, bm, batch, seq):
    return pl.pallas_call(
        _attn_body,
        grid=(batch, HEADS),
        in_specs=[pl.BlockSpec((seq, DH), lambda b, h: (b, h)),
                  pl.BlockSpec((seq, DH), lambda b, h: (b, HEADS + h)),
                  pl.BlockSpec((seq, DH), lambda b, h: (b, 2 * HEADS + h)),
                  pl.BlockSpec((1, QB, KWIN), lambda b, h: (h, 0, 0))],
        out_specs=pl.BlockSpec((seq, DH), lambda b, h: (b, h)),
        out_shape=jax.ShapeDtypeStruct((batch * seq, ATT_W), jnp.bfloat16),
        compiler_params=_cp(("arbitrary", "arbitrary")),
        name="band_attn",
    )(z, z, z, bm)


def _sgu_body(u_ref, v_ref, g_ref, b_ref, w_ref, bs_ref, o_ref):
    t = u_ref.shape[0]
    u = u_ref[...].astype(jnp.float32)
    vn = _ln(v_ref[...].astype(jnp.float32), g_ref[...], b_ref[...]).astype(jnp.bfloat16)
    row = lax.broadcasted_iota(jnp.int32, (WIN, WIN), 0)
    col = lax.broadcasted_iota(jnp.int32, (WIN, WIN), 1)
    causal = row >= col
    for g in range(SGU_G):
        wm = jnp.where(causal, w_ref[g], 0.0).astype(jnp.bfloat16)
        bias = bs_ref[:, g:g + 1]
        for n in range(t // WIN):
            r0, c0 = n * WIN, g * 128
            mixed = jnp.dot(wm, vn[r0:r0 + WIN, c0:c0 + 128],
                            preferred_element_type=jnp.float32) + bias
            o_ref[r0:r0 + WIN, c0:c0 + 128] = (u[r0:r0 + WIN, c0:c0 + 128] * mixed
                                               ).astype(o_ref.dtype)


def _sgu(z, ln_g, ln_b, w_s, b_s):
    n = z.shape[0]
    t = 512
    ublk = 3 * ATT_W // SGU_W
    return pl.pallas_call(
        _sgu_body,
        grid=(n // t,),
        in_specs=[pl.BlockSpec((t, SGU_W), lambda i: (i, ublk)),
                  pl.BlockSpec((t, SGU_W), lambda i: (i, ublk + 1)),
                  pl.BlockSpec((1, SGU_W), lambda i: (0, 0)),
                  pl.BlockSpec((1, SGU_W), lambda i: (0, 0)),
                  pl.BlockSpec((SGU_G, WIN, WIN), lambda i: (0, 0, 0)),
                  pl.BlockSpec((WIN, SGU_G), lambda i: (0, 0))],
        out_specs=pl.BlockSpec((t, SGU_W), lambda i: (i, 0)),
        out_shape=jax.ShapeDtypeStruct((n, SGU_W), jnp.bfloat16),
        compiler_params=_cp(("arbitrary",)),
        name="sgu",
    )(z, z, ln_g.reshape(1, SGU_W), ln_b.reshape(1, SGU_W), w_s, b_s.T)


def _top2_of4(a, b, c, d):
    m1, n1 = jnp.maximum(a, b), jnp.minimum(a, b)
    m2, n2 = jnp.maximum(c, d), jnp.minimum(c, d)
    top = jnp.maximum(m1, m2)
    second = jnp.maximum(jnp.minimum(m1, m2), jnp.maximum(n1, n2))
    return top + second


def _route_body(l_ref, id_ref, gt_ref):
    rows = [l_ref[e:e + 1, :] for e in range(N_EXP)]
    mx = functools.reduce(jnp.maximum, rows)
    ex = [jnp.exp(r - mx) for r in rows]
    den = functools.reduce(lambda a, b: a + b, ex)
    p = [e / den for e in ex]
    score = [_top2_of4(*p[EPG * g:EPG * (g + 1)]) for g in range(N_GRP)]
    best, sel = score[0], jnp.zeros_like(score[0], dtype=jnp.int32)
    for g in range(1, N_GRP):
        better = score[g] > best
        best = jnp.where(better, score[g], best)
        sel = jnp.where(better, g, sel)
    cand = []
    for k in range(EPG):
        val = p[k]
        for g in range(1, N_GRP):
            val = jnp.where(sel == g, p[EPG * g + k], val)
        cand.append(val)
    p1, i1 = cand[0], jnp.zeros_like(sel)
    for k in range(1, EPG):
        better = cand[k] > p1
        p1 = jnp.where(better, cand[k], p1)
        i1 = jnp.where(better, k, i1)
    p2, i2 = jnp.full_like(p1, -1.0), jnp.zeros_like(sel)
    for k in range(EPG):
        better = (cand[k] > p2) & (i1 != k)
        p2 = jnp.where(better, cand[k], p2)
        i2 = jnp.where(better, k, i2)
    tot = p1 + p2
    id_ref[...] = jnp.concatenate([sel * EPG + i1, sel * EPG + i2], axis=0)
    pad = jnp.zeros((SUBLANES - 2, p1.shape[1]), jnp.float32)
    gt_ref[...] = jnp.concatenate([p1 / tot, p2 / tot, pad], axis=0)


def _route(logits):
    n = logits.shape[1]
    return pl.pallas_call(
        _route_body,
        out_shape=[jax.ShapeDtypeStruct((2, n), jnp.int32),
                   jax.ShapeDtypeStruct((SUBLANES, n), jnp.float32)],
        compiler_params=_cp(None),
        name="route",
    )(logits)


def _outproj_body(ya_ref, ys_ref, w_ref, h_ref, gin_ref, bin_ref, g_ref, b_ref,
                  rw2_ref, rb_ref, hf_ref, hp_ref, lg_ref, *, raw_input):
    mix = jnp.dot(ya_ref[...], w_ref[:ATT_W, :], preferred_element_type=jnp.float32)
    mix = mix + jnp.dot(ys_ref[...], w_ref[ATT_W:, :], preferred_element_type=jnp.float32)
    h = h_ref[...]
    if raw_input:
        h = _ln(h, gin_ref[...], bin_ref[...])
    h1 = _ln(ALPHA * h + mix, g_ref[...], b_ref[...])
    hf_ref[...] = h1
    _pack_rows(h1, hp_ref)
    hh = h1.astype(jnp.bfloat16)
    hl = (h1 - hh.astype(jnp.float32)).astype(jnp.bfloat16)
    nt = (((1,), (1,)), ((), ()))
    both = lax.dot_general(rw2_ref[...], hh, nt, preferred_element_type=jnp.float32)
    low = lax.dot_general(rw2_ref[:N_EXP, :], hl, nt, preferred_element_type=jnp.float32)
    lg_ref[...] = both[:N_EXP] + low + both[N_EXP:] + rb_ref[...]


def _outproj(ya, ys, wb, resid, gin, bin_, g, b, rw2, rb, raw_input):
    n = ya.shape[0]
    t = 512
    return pl.pallas_call(
        functools.partial(_outproj_body, raw_input=raw_input),
        grid=(n // t,),
        in_specs=[pl.BlockSpec((t, ATT_W), lambda i: (i, 0)),
                  pl.BlockSpec((t, SGU_W), lambda i: (i, 0)),
                  pl.BlockSpec((D, D), lambda i: (0, 0)),
                  pl.BlockSpec((t, D), lambda i: (i, 0)),
                  pl.BlockSpec((1, D), lambda i: (0, 0)),
                  pl.BlockSpec((1, D), lambda i: (0, 0)),
                  pl.BlockSpec((1, D), lambda i: (0, 0)),
                  pl.BlockSpec((1, D), lambda i: (0, 0)),
                  pl.BlockSpec((2 * N_EXP, D), lambda i: (0, 0)),
                  pl.BlockSpec((N_EXP, 1), lambda i: (0, 0))],
        out_specs=[pl.BlockSpec((t, D), lambda i: (i, 0)),
                   pl.BlockSpec((t * PITCH, LANES), lambda i: (i, 0)),
                   pl.BlockSpec((N_EXP, t), lambda i: (0, i))],
        out_shape=[jax.ShapeDtypeStruct((n, D), jnp.float32),
                   jax.ShapeDtypeStruct((n * PITCH, LANES), jnp.float32),
                   jax.ShapeDtypeStruct((N_EXP, n), jnp.float32)],
        compiler_params=_cp(("arbitrary",)),
        name="out_proj_ln_route",
    )(ya, ys, wb, resid, gin.reshape(1, D), bin_.reshape(1, D), g.reshape(1, D),
      b.reshape(1, D), rw2, rb)


def _weight_copies(pairs, layer, e, sem):
    copies = []
    for w_ref, stage, _ in pairs:
        rows = stage.shape[0] // W_CHUNKS
        for c in range(W_CHUNKS):
            copies.append(pltpu.make_async_copy(w_ref.at[layer, e, pl.ds(c * rows, rows)],
                                                stage.at[pl.ds(c * rows, rows)], sem))
    return copies


def _refresh_weights(pairs, layer, te_ref, tend_ref, nt, t, sem):
    e = te_ref[t]

    @pl.when((t == 0) | (e != te_ref[jnp.maximum(t - 1, 0)]))
    def _():
        for cp in _weight_copies(pairs, layer, e, sem):
            cp.wait()
        for _, stage, work in pairs:
            def cast(c, carry, stage=stage, work=work):
                rows = pl.ds(pl.multiple_of(c * CAST_ROWS, CAST_ROWS), CAST_ROWS)
                work[rows, :] = stage[rows, :].astype(jnp.bfloat16)
                return carry

            lax.fori_loop(0, stage.shape[0] // CAST_ROWS, cast, 0)
        nxt = tend_ref[e]

        @pl.when(nxt < nt)
        def _():
            e_next = te_ref[jnp.minimum(nxt, te_ref.shape[0] - 1)]
            for cp in _weight_copies(pairs, layer, e_next, sem):
                cp.start(priority=W_PRIORITY)


def _experts_body(te_ref, nt_ref, tend_ref, nvalid_ref, pos_ref,
                  hp_ref, wg_ref, wu_ref, wd_ref, y_ref,
                  xa, xb, ya, yb, wgs, wus, wds, wgb, wub, wdb, asg_ref, sems, *, layer, n):
    step = pl.program_id(0)
    nt = nt_ref[0]
    gsem = (sems.at[0], sems.at[1])
    ssem = (sems.at[2], sems.at[3])
    wsem = sems.at[4]
    pairs = ((wg_ref, wgs, wgb), (wu_ref, wus, wub), (wd_ref, wds, wdb))
    tile_rows = TM * PITCH

    def assignment(tile, r):
        valid = r < nvalid_ref[tile]
        return valid, asg_ref[jnp.where(valid, tile * TM + r, 0)]

    def gather_copy(tile, r, xbuf, sem):
        valid, a = assignment(tile, r)
        tok = jnp.where(valid, jnp.where(a >= n, a - n, a), 0)
        return pltpu.make_async_copy(hp_ref.at[pl.ds(tok * PITCH, PITCH)],
                                     xbuf.at[pl.ds(r * PITCH, PITCH)], sem)

    def scatter_copy(tile, r, ybuf, sem, dump_block):
        valid, a = assignment(jnp.maximum(tile, 0), r)
        dst = jnp.where(valid & (tile >= 0), a, 2 * n + dump_block * TM + r)
        return pltpu.make_async_copy(ybuf.at[pl.ds(r * PITCH, PITCH)],
                                     y_ref.at[pl.ds(dst * PITCH, PITCH)], sem)

    def wait_gather(xbuf, sem):
        pltpu.make_async_copy(hp_ref.at[pl.ds(0, tile_rows)], xbuf, sem).wait()

    def wait_scatter(ybuf, sem):
        pltpu.make_async_copy(ybuf, y_ref.at[pl.ds(0, tile_rows)], sem).wait()

    @pl.when(step == 0)
    def _():
        for cp in _weight_copies(pairs, layer, te_ref[0], wsem):
            cp.start(priority=W_PRIORITY)

        def invert(a, carry):
            asg_ref[pos_ref[a]] = a
            return carry

        lax.fori_loop(0, 2 * n, invert, 0, unroll=32)

        def issue(r, carry):
            gather_copy(0, r, xa, gsem[0]).start()
            return carry

        lax.fori_loop(0, TM, issue, 0)
        ya[...] = jnp.zeros_like(ya)
        yb[...] = jnp.zeros_like(yb)
        pltpu.make_async_copy(ya, y_ref.at[pl.ds(2 * n * PITCH, tile_rows)], ssem[0]).start()

    def phase(t, xcur, gcur, xnext, gnext, ycur, scur, yprev, sprev, cur_block):
        @pl.when(t < nt)
        def _():
            _refresh_weights(pairs, layer, te_ref, tend_ref, nt, t, wsem)
            wait_gather(xcur, gcur)
            tnext = jnp.minimum(t + 1, nt - 1)
            for r in range(TM):
                gather_copy(tnext, r, xnext, gnext).start()
            for r in range(TM):
                scatter_copy(t - 1, r, yprev, sprev, 1 - cur_block).start()
            x = jnp.concatenate([c.astype(jnp.bfloat16) for c in _unpack_rows(xcur, TM)],
                                axis=1)
            g = jnp.dot(x, wgb[...], preferred_element_type=jnp.float32)
            u = jnp.dot(x, wub[...], preferred_element_type=jnp.float32)
            hid = (g * jax.nn.sigmoid(g) * u).astype(jnp.bfloat16)
            y = jnp.dot(hid, wdb[...], preferred_element_type=jnp.float32)
            wait_scatter(ycur, scur)
            _pack_rows(y, ycur)

            @pl.when(t == nt - 1)
            def _():
                wait_gather(xnext, gnext)
                wait_scatter(yprev, sprev)

                def issue(r, carry):
                    scatter_copy(t, r, ycur, scur, cur_block).start()
                    return carry

                lax.fori_loop(0, TM, issue, 0)
                wait_scatter(ycur, scur)

    phase(2 * step, xa, gsem[0], xb, gsem[1], ya, ssem[0], yb, ssem[1], 0)
    phase(2 * step + 1, xb, gsem[1], xa, gsem[0], yb, ssem[1], ya, ssem[0], 1)


def _experts(tile_expert, ntiles, tile_end, nvalid, pos, hp, w_gate, w_up, w_down,
             layer, n, max_tiles):
    any_spec = pl.BlockSpec(memory_space=pl.ANY)
    slab = pltpu.VMEM((TM * PITCH, LANES), jnp.float32)
    return pl.pallas_call(
        functools.partial(_experts_body, layer=layer, n=n),
        grid_spec=pltpu.PrefetchScalarGridSpec(
            num_scalar_prefetch=5,
            grid=(max_tiles // 2,),
            in_specs=[any_spec, any_spec, any_spec, any_spec],
            out_specs=any_spec,
            scratch_shapes=[slab, slab, slab, slab,
                            pltpu.VMEM((D, F), jnp.float32),
                            pltpu.VMEM((D, F), jnp.float32),
                            pltpu.VMEM((F, D), jnp.float32),
                            pltpu.VMEM((D, F), jnp.bfloat16),
                            pltpu.VMEM((D, F), jnp.bfloat16),
                            pltpu.VMEM((F, D), jnp.bfloat16),
                            pltpu.SMEM((max_tiles * TM,), jnp.int32),
                            pltpu.SemaphoreType.DMA((5,))]),
        out_shape=jax.ShapeDtypeStruct(((2 * n + 2 * TM) * PITCH, LANES), jnp.float32),
        compiler_params=_cp(("arbitrary",), disable_bounds_checks=True),
        name="experts",
    )(tile_expert, ntiles, tile_end, nvalid, pos, hp, w_gate, w_up, w_down)


def _combine_body(y0_ref, y1_ref, h_ref, gt_ref, g_ref, b_ref, hf_ref, *maybe_hb_ref):
    t = h_ref.shape[0]
    y0 = jnp.concatenate(_unpack_rows(y0_ref, t), axis=1)
    y1 = jnp.concatenate(_unpack_rows(y1_ref, t), axis=1)
    gt = jnp.transpose(gt_ref[...])
    ffn = gt[:, 0:1] * y0 + gt[:, 1:2] * y1
    h2 = _ln(ALPHA * h_ref[...] + ffn, g_ref[...], b_ref[...])
    hf_ref[...] = h2
    for hb_ref in maybe_hb_ref:
        hb_ref[...] = h2.astype(jnp.bfloat16)


def _combine(yg, hf, gates, g, b, emit_bf16):
    n = hf.shape[0]
    t = 512
    nb = n // t
    row_spec = pl.BlockSpec((t, D), lambda i: (i, 0))
    out_specs = [row_spec] + ([row_spec] if emit_bf16 else [])
    out_shape = [jax.ShapeDtypeStruct((n, D), jnp.float32)]
    if emit_bf16:
        out_shape.append(jax.ShapeDtypeStruct((n, D), jnp.bfloat16))
    return pl.pallas_call(
        _combine_body,
        grid=(nb,),
        in_specs=[pl.BlockSpec((t * PITCH, LANES), lambda i: (i, 0)),
                  pl.BlockSpec((t * PITCH, LANES), lambda i: (nb + i, 0)),
                  row_spec,
                  pl.BlockSpec((SUBLANES, t), lambda i: (0, i)),
                  pl.BlockSpec((1, D), lambda i: (0, 0)),
                  pl.BlockSpec((1, D), lambda i: (0, 0))],
        out_specs=out_specs,
        out_shape=out_shape,
        compiler_params=_cp(("arbitrary",)),
        name="combine_ln",
    )(yg, yg, hf, gates, g.reshape(1, D), b.reshape(1, D))


def _dispatch_plan(ids, n, max_tiles):
    i32 = jnp.int32
    e_flat = ids.reshape(-1)
    experts = jnp.arange(N_EXP, dtype=i32)
    onehot = (e_flat[:, None] == experts[None, :]).astype(i32)
    csum = jnp.cumsum(onehot, axis=0)
    counts = csum[-1]
    tiles = (counts + TM - 1) // TM
    tile_end = jnp.cumsum(tiles).astype(i32)
    tile_start = tile_end - tiles
    pos = jnp.sum(onehot * (csum - 1 + (tile_start * TM)[None, :]), axis=1).astype(i32)
    tile_ids = jnp.arange(max_tiles, dtype=i32)
    tile_expert = jnp.minimum(
        jnp.sum((tile_ids[:, None] >= tile_end[None, :]).astype(i32), axis=1), N_EXP - 1)
    mine = (tile_expert[:, None] == experts[None, :]).astype(i32)
    before = (tile_ids - jnp.sum(mine * tile_start[None, :], axis=1)) * TM
    nvalid = jnp.clip(jnp.sum(mine * counts[None, :], axis=1) - before, 0, TM).astype(i32)
    return pos, nvalid, tile_expert, tile_end[-1:], tile_end


def kernel(x, ln_in_g, ln_in_b, w_in, rel_bias, sgu_ln_g, sgu_ln_b, sgu_w, sgu_b, w_out,
           ln1_g, ln1_b, router_w, router_b, w_gate, w_up, w_down, ln2_g, ln2_b):
    batch, seq, _ = x.shape
    n = batch * seq
    max_tiles = (2 * n) // TM + N_EXP
    rwt = router_w.T
    rwh = rwt.astype(jnp.bfloat16)
    rwl = (rwt - rwh.astype(jnp.float32)).astype(jnp.bfloat16)
    rw2 = jnp.concatenate([rwh, rwl], axis=0)
    rb = router_b.reshape(N_EXP, 1)

    hf = x.reshape(n, D)
    hb = _ln0(hf, ln_in_g, ln_in_b)
    for l in range(DEPTH):
        z = _inproj(hb, w_in, l)
        ya = _attention(z, _attn_bias(rel_bias[l]), batch, seq)
        ys = _sgu(z, sgu_ln_g[l], sgu_ln_b[l], sgu_w[l], sgu_b[l])
        hf, hp, logits = _outproj(ya, ys, w_out[l].astype(jnp.bfloat16), hf,
                                  ln_in_g, ln_in_b, ln1_g[l], ln1_b[l], rw2, rb,
                                  raw_input=(l == 0))
        ids, gates = _route(logits)
        pos, nvalid, tile_expert, ntiles, tile_end = _dispatch_plan(ids, n, max_tiles)
        yslots = _experts(tile_expert, ntiles, tile_end, nvalid, pos, hp,
                          w_gate, w_up, w_down, l, n, max_tiles)
        outs = _combine(yslots, hf, gates, ln2_g[l], ln2_b[l], emit_bf16=(l + 1 < DEPTH))
        hf = outs[0]
        hb = outs[1] if l + 1 < DEPTH else None
    return hf.reshape(batch, seq, D)
```
